```python
import math
import jax
import jax.numpy as jnp
from jax import lax
import numpy as np

D_MODEL = 4096
BATCH = 4
SEQ = 2048
DEPTH = 2

HG_HEADS = 8
HG_DK = 128
HG_DV = 128
HG_WIDTH = HG_HEADS * HG_DV
HG_CHUNK = 64
POOL_WINDOWS = (2, 4, 8, 16)
POOL_GROUPS = 4
POOL_GROUP_DIM = 256
POOL_WIDTH = POOL_GROUPS * POOL_GROUP_DIM
DA_HEADS = 8
DA_HEAD_DIM = 64
DA_VDIM = 2 * DA_HEAD_DIM
DA_WIDTH = DA_HEADS * DA_VDIM
Q_BLOCK = 128
N_BRANCH = 3
FFN_HIDDEN = -(-8 * D_MODEL // (3 * 256)) * 256
NORM_EPS = 1e-6

IN_SPLITS = (
    HG_HEADS * HG_DK,
    HG_HEADS * HG_DK,
    HG_WIDTH,
    HG_WIDTH,
    POOL_WIDTH,
    DA_HEADS * 2 * DA_HEAD_DIM,
    DA_HEADS * 2 * DA_HEAD_DIM,
    DA_WIDTH,
    N_BRANCH * D_MODEL,
)
IN_COLS = sum(IN_SPLITS)

kernel_name = "hybrid_hgrn2_pool_diffattn_gated_block"


def rms_norm(x, gain):
    xf = x.astype(jnp.float32)
    y = xf * lax.rsqrt(jnp.mean(xf * xf, axis=-1, keepdims=True) + NORM_EPS)
    return (y * gain.astype(jnp.float32)).astype(x.dtype)


def split_columns(proj):
    pieces, start = [], 0
    for size in IN_SPLITS:
        pieces.append(proj[..., start:start + size])
        start += size
    return pieces


def alibi_slopes(n_heads):
    return 2.0 ** (-8.0 * jnp.arange(1, n_heads + 1, dtype=jnp.float32) / n_heads)


def hgrn2_mixer(q, f_raw, v, g, lb, out_gain):
    B, S, _ = q.shape
    nc = S // HG_CHUNK
    f = lb.astype(jnp.float32) + (1.0 - lb.astype(jnp.float32)) * jax.nn.sigmoid(f_raw.astype(jnp.float32))
    log_f = jnp.log(f)
    k = 1.0 - f

    def to_chunks(t, d):
        return t.astype(jnp.float32).reshape(B, nc, HG_CHUNK, HG_HEADS, d).transpose(1, 0, 3, 2, 4)

    qc = to_chunks(q, HG_DK) * (HG_DK ** -0.5)
    kc = to_chunks(k, HG_DK)
    gc = to_chunks(log_f, HG_DK)
    vc = to_chunks(v, HG_DV)
    causal = jnp.tril(jnp.ones((HG_CHUNK, HG_CHUNK), dtype=bool))

    def chunk_step(state, inp):
        qb, kb, vb, gb = inp
        b = jnp.cumsum(gb, axis=2)
        o_inter = jnp.einsum('bhtk,bhkv->bhtv', qb * jnp.exp(b), state)
        diff = b[:, :, :, None, :] - b[:, :, None, :, :]
        decay = jnp.exp(jnp.where(causal[:, :, None], diff, -jnp.inf))
        scores = jnp.sum(qb[:, :, :, None, :] * kb[:, :, None, :, :] * decay, axis=-1)
        o_intra = jnp.einsum('bhts,bhsv->bhtv', scores, vb)
        b_last = b[:, :, -1:, :]
        new_state = jnp.exp(b_last[:, :, 0, :])[..., None] * state + jnp.einsum(
            'bhsk,bhsv->bhkv', kb * jnp.exp(b_last - b), vb)
        return new_state, o_inter + o_intra

    state0 = jnp.zeros((B, HG_HEADS, HG_DK, HG_DV), jnp.float32)
    _, o = lax.scan(chunk_step, state0, (qc, kc, vc, gc))
    o = o.transpose(1, 0, 3, 2, 4).reshape(B, S, HG_HEADS, HG_DV)
    o = o * lax.rsqrt(jnp.mean(o * o, axis=-1, keepdims=True) + NORM_EPS)
    o = o.reshape(B, S, HG_WIDTH) * out_gain.astype(jnp.float32)
    return (o * jax.nn.silu(g.astype(jnp.float32))).astype(q.dtype)


def pool_mixer(u, w_groups, scale):
    B, S, _ = u.shape
    uf = u.astype(jnp.float32).reshape(B, S, POOL_GROUPS, POOL_GROUP_DIM)
    cs = jnp.concatenate([jnp.zeros((B, 1, POOL_GROUPS, POOL_GROUP_DIM), jnp.float32),
                          jnp.cumsum(uf, axis=1)], axis=1)
    t = jnp.arange(S)
    outs = []
    for j, w in enumerate(POOL_WINDOWS):
        start = jnp.maximum(t + 1 - w, 0)
        win_sum = cs[:, t + 1, j] - cs[:, start, j]
        count = (t + 1 - start).astype(jnp.float32)[None, :, None]
        outs.append(win_sum / count - uf[:, :, j])
    pooled = jnp.stack(outs, axis=2)
    mixed = jnp.einsum('bsgc,gcd->bsgd', pooled, w_groups.astype(jnp.float32))
    return (mixed.reshape(B, S, POOL_WIDTH) * scale.astype(jnp.float32)).astype(u.dtype)


def diff_attention(q, k, v, lam_params, subln_gain, lambda_init):
    B, S, _ = q.shape
    nb = S // Q_BLOCK
    qf = q.astype(jnp.float32).reshape(B, S, DA_HEADS, 2, DA_HEAD_DIM) * (DA_HEAD_DIM ** -0.5)
    kf = k.astype(jnp.float32).reshape(B, S, DA_HEADS, 2, DA_HEAD_DIM)
    vf = v.astype(jnp.float32).reshape(B, S, DA_HEADS, DA_VDIM)
    lp = lam_params.astype(jnp.float32)
    lam = jnp.exp(jnp.sum(lp[0] * lp[1])) - jnp.exp(jnp.sum(lp[2] * lp[3])) + lambda_init
    slopes = alibi_slopes(DA_HEADS)
    key_pos = jnp.arange(S)
    q_blocks = qf.reshape(B, nb, Q_BLOCK, DA_HEADS, 2, DA_HEAD_DIM).transpose(1, 0, 2, 3, 4, 5)

    def one_block(args):
        q_blk, blk = args
        q_pos = blk * Q_BLOCK + jnp.arange(Q_BLOCK)
        dist = (q_pos[:, None] - key_pos[None, :]).astype(jnp.float32)
        bias = -slopes[:, None, None] * dist
        s = jnp.einsum('bqhmd,bkhmd->bhmqk', q_blk, kf) + bias[None, :, None]
        s = jnp.where(dist >= 0, s, -jnp.inf)
        p = jax.nn.softmax(s, axis=-1)
        a = p[:, :, 0] - lam * p[:, :, 1]
        return jnp.einsum('bhqk,bkhv->bqhv', a, vf)

    o = lax.map(one_block, (q_blocks, jnp.arange(nb)))
    o = o.transpose(1, 0, 2, 3, 4).reshape(B, S, DA_HEADS, DA_VDIM)
    o = o * lax.rsqrt(jnp.mean(o * o, axis=-1, keepdims=True) + NORM_EPS) * subln_gain.astype(jnp.float32)
    o = o * (1.0 - lambda_init)
    return o.reshape(B, S, DA_WIDTH).astype(q.dtype)


def setup_inputs(seed: int = 0) -> dict:
    key = jax.random.key(seed)
    ks = jax.random.split(key, 20)
    L, D, F = DEPTH, D_MODEL, FFN_HIDDEN

    def nrm(k, shape, scale):
        return jax.random.normal(k, shape, jnp.float32) * scale

    def gain(k, shape):
        return 1.0 + 0.02 * jax.random.normal(k, shape, jnp.float32)

    return {
        "x": nrm(ks[0], (BATCH, SEQ, D), 1.0),
        "norm_mix_pre": gain(ks[1], (L, D)),
        "norm_mix_post": gain(ks[2], (L, D)),
        "norm_ffn_pre": gain(ks[3], (L, D)),
        "norm_ffn_post": gain(ks[4], (L, D)),
        "w_in": nrm(ks[5], (L, D, IN_COLS), D ** -0.5),
        "hgrn_lb_logits": nrm(ks[6], (L, HG_HEADS * HG_DK), 0.5),
        "hgrn_out_norm": gain(ks[7], (L, HG_WIDTH)),
        "pool_w": nrm(ks[8], (L, POOL_GROUPS, POOL_GROUP_DIM, POOL_GROUP_DIM), POOL_GROUP_DIM ** -0.5),
        "pool_scale": gain(ks[9], (L, POOL_WIDTH)),
        "diff_lambda": nrm(ks[10], (L, 4, DA_HEAD_DIM), 0.1),
        "diff_subln": gain(ks[11], (L, DA_VDIM)),
        "w_up_a": nrm(ks[12], (L, HG_WIDTH, D), HG_WIDTH ** -0.5),
        "w_up_b": nrm(ks[13], (L, POOL_WIDTH, D), POOL_WIDTH ** -0.5),
        "w_up_c": nrm(ks[14], (L, DA_WIDTH, D), DA_WIDTH ** -0.5),
        "w_out": nrm(ks[15], (L, D, D), D ** -0.5),
        "w_ffn_gate": nrm(ks[16], (L, D, F), D ** -0.5),
        "w_ffn_up": nrm(ks[17], (L, D, F), D ** -0.5),
        "w_ffn_down": nrm(ks[18], (L, F, D), F ** -0.5),
    }


def reference(x, norm_mix_pre, norm_mix_post, norm_ffn_pre, norm_ffn_post, w_in,
              hgrn_lb_logits, hgrn_out_norm, pool_w, pool_scale, diff_lambda, diff_subln,
              w_up_a, w_up_b, w_up_c, w_out, w_ffn_gate, w_ffn_up, w_ffn_down):
    B, S, D = x.shape
    lb_all = jnp.cumsum(jax.nn.softmax(hgrn_lb_logits.astype(jnp.float32), axis=0), axis=0)
    lb_all = lb_all - lb_all[0:1]
    for l in range(DEPTH):
        lambda_init = 0.8 - 0.6 * math.exp(-0.3 * l)
        h = rms_norm(x, norm_mix_pre[l])
        proj = h @ w_in[l]
        hq, hf, hv, hg, pu, dq, dk, dv, gate_logits = split_columns(proj)
        y_a = hgrn2_mixer(hq, hf, hv, hg, lb_all[l], hgrn_out_norm[l])
        y_b = pool_mixer(pu, pool_w[l], pool_scale[l])
        y_c = diff_attention(dq, dk, dv, diff_lambda[l], diff_subln[l], lambda_init)
        gates = jax.nn.sigmoid(gate_logits.astype(jnp.float32)).reshape(B, S, N_BRANCH, D)
        merged = (gates[:, :, 0] * (y_a @ w_up_a[l])
                  + gates[:, :, 1] * (y_b @ w_up_b[l])
                  + gates[:, :, 2] * (y_c @ w_up_c[l])).astype(x.dtype)
        x = x + rms_norm(merged @ w_out[l], norm_mix_post[l])
        h = rms_norm(x, norm_ffn_pre[l])
        ff = (jax.nn.silu(h @ w_ffn_gate[l]) * (h @ w_ffn_up[l])) @ w_ffn_down[l]
        x = x + rms_norm(ff, norm_ffn_post[l])
    return x
```

```python
import functools
import math

import jax
import jax.numpy as jnp
import numpy as np
from jax import lax
from jax.experimental import pallas as pl
from jax.experimental.pallas import tpu as pltpu

HG_HEADS = 8
HG_DK = 128
HG_DV = 128
HG_WIDTH = HG_HEADS * HG_DV
HG_CHUNK = 64
POOL_WINDOWS = (2, 4, 8, 16)
POOL_GROUPS = 4
POOL_GROUP_DIM = 256
POOL_WIDTH = POOL_GROUPS * POOL_GROUP_DIM
DA_HEADS = 8
DA_HEAD_DIM = 64
DA_VDIM = 2 * DA_HEAD_DIM
DA_WIDTH = DA_HEADS * DA_VDIM
N_BRANCH = 3
NORM_EPS = 1e-6

OFF_HQ = 0
OFF_HF = OFF_HQ + HG_HEADS * HG_DK
OFF_HV = OFF_HF + HG_HEADS * HG_DK
OFF_HG = OFF_HV + HG_WIDTH
OFF_PU = OFF_HG + HG_WIDTH
OFF_DQ = OFF_PU + POOL_WIDTH
OFF_DK = OFF_DQ + DA_WIDTH
OFF_DV = OFF_DK + DA_WIDTH
OFF_GATE = OFF_DV + DA_WIDTH

LANES = 128
V7X_VMEM_LIMIT_BYTES = 56 * 1024 * 1024

BF16 = jnp.bfloat16
F32 = jnp.float32


def _cparams(*sem):
    return pltpu.CompilerParams(dimension_semantics=sem, vmem_limit_bytes=V7X_VMEM_LIMIT_BYTES)


def _rms(x, gain):
    return x * lax.rsqrt(jnp.mean(x * x, axis=-1, keepdims=True) + NORM_EPS) * gain


def _prenorm_kernel(x_ref, g_ref, h_ref):
    h_ref[...] = _rms(x_ref[...], g_ref[...]).astype(h_ref.dtype)


def prenorm(x2, gain, *, rows=256):
    T, D = x2.shape
    return pl.pallas_call(
        _prenorm_kernel,
        grid=(T // rows,),
        in_specs=[pl.BlockSpec((rows, D), lambda i: (i, 0)),
                  pl.BlockSpec((1, D), lambda i: (0, 0))],
        out_specs=pl.BlockSpec((rows, D), lambda i: (i, 0)),
        out_shape=jax.ShapeDtypeStruct((T, D), BF16),
        compiler_params=_cparams("parallel"),
        name="prenorm",
    )(x2, gain.reshape(1, D))


def _postnorm_kernel(x_ref, z_ref, gpost_ref, gnext_ref, xo_ref, h_ref):
    xn = x_ref[...] + _rms(z_ref[...], gpost_ref[...])
    xo_ref[...] = xn
    h_ref[...] = _rms(xn, gnext_ref[...]).astype(h_ref.dtype)


def _postnorm_last_kernel(x_ref, z_ref, gpost_ref, xo_ref):
    xo_ref[...] = x_ref[...] + _rms(z_ref[...], gpost_ref[...])


def postnorm(x2, z, gpost, gnext, *, rows=256):
    T, D = x2.shape
    row_spec = pl.BlockSpec((rows, D), lambda i: (i, 0))
    vec_spec = pl.BlockSpec((1, D), lambda i: (0, 0))
    if gnext is None:
        return pl.pallas_call(
            _postnorm_last_kernel,
            grid=(T // rows,),
            in_specs=[row_spec, row_spec, vec_spec],
            out_specs=row_spec,
            out_shape=jax.ShapeDtypeStruct((T, D), F32),
            compiler_params=_cparams("parallel"),
            name="postnorm_last",
        )(x2, z, gpost.reshape(1, D)), None
    return pl.pallas_call(
        _postnorm_kernel,
        grid=(T // rows,),
        in_specs=[row_spec, row_spec, vec_spec, vec_spec],
        out_specs=[row_spec, row_spec],
        out_shape=[jax.ShapeDtypeStruct((T, D), F32), jax.ShapeDtypeStruct((T, D), BF16)],
        compiler_params=_cparams("parallel"),
        name="postnorm",
    )(x2, z, gpost.reshape(1, D), gnext.reshape(1, D))


def _mm_kernel(a_ref, w_ref, o_ref):
    o_ref[...] = jnp.dot(a_ref[...], w_ref[...], preferred_element_type=F32).astype(o_ref.dtype)


def matmul(a, w, *, bm, bn, out_dtype, name):
    M, K = a.shape
    N = w.shape[1]
    return pl.pallas_call(
        _mm_kernel,
        grid=(M // bm, N // bn),
        in_specs=[pl.BlockSpec((bm, K), lambda i, j: (i, 0)),
                  pl.BlockSpec((K, bn), lambda i, j: (0, j))],
        out_specs=pl.BlockSpec((bm, bn), lambda i, j: (i, j)),
        out_shape=jax.ShapeDtypeStruct((M, N), out_dtype),
        compiler_params=_cparams("parallel", "arbitrary"),
        name=name,
    )(a, w)


def _swiglu_kernel(h_ref, wg_ref, wu_ref, o_ref):
    h = h_ref[...]
    a = jnp.dot(h, wg_ref[...], preferred_element_type=F32)
    b = jnp.dot(h, wu_ref[...], preferred_element_type=F32)
    o_ref[...] = (a * jax.nn.sigmoid(a) * b).astype(o_ref.dtype)


def swiglu_up(h, wg, wu, *, bm, bn):
    M, K = h.shape
    N = wg.shape[1]
    w_spec = pl.BlockSpec((K, bn), lambda i, j: (0, j))
    return pl.pallas_call(
        _swiglu_kernel,
        grid=(M // bm, N // bn),
        in_specs=[pl.BlockSpec((bm, K), lambda i, j: (i, 0)), w_spec, w_spec],
        out_specs=pl.BlockSpec((bm, bn), lambda i, j: (i, j)),
        out_shape=jax.ShapeDtypeStruct((M, N), BF16),
        compiler_params=_cparams("parallel", "arbitrary"),
        name="swiglu_up",
    )(h, wg, wu)


def _merge_kernel(ya_ref, yb_ref, yc_ref, wa_ref, wb_ref, wc_ref, ga_ref, gb_ref, gc_ref, o_ref):
    acc = jax.nn.sigmoid(ga_ref[...]) * jnp.dot(ya_ref[...], wa_ref[...], preferred_element_type=F32)
    acc += jax.nn.sigmoid(gb_ref[...]) * jnp.dot(yb_ref[...], wb_ref[...], preferred_element_type=F32)
    acc += jax.nn.sigmoid(gc_ref[...]) * jnp.dot(yc_ref[...], wc_ref[...], preferred_element_type=F32)
    o_ref[...] = acc.astype(o_ref.dtype)


def gated_merge(ya, yb, yc, wa, wb, wc, proj, d_model, *, bm, bn):
    M, Kb = ya.shape
    gate_blk = OFF_GATE // bn
    per_branch = d_model // bn
    y_spec = pl.BlockSpec((bm, Kb), lambda i, j: (i, 0))
    w_spec = pl.BlockSpec((Kb, bn), lambda i, j: (0, j))

    def g_spec(k):
        return pl.BlockSpec((bm, bn), lambda i, j: (i, gate_blk + k * per_branch + j))

    return pl.pallas_call(
        _merge_kernel,
        grid=(M // bm, d_model // bn),
        in_specs=[y_spec, y_spec, y_spec, w_spec, w_spec, w_spec, g_spec(0), g_spec(1), g_spec(2)],
        out_specs=pl.BlockSpec((bm, bn), lambda i, j: (i, j)),
        out_shape=jax.ShapeDtypeStruct((M, d_model), BF16),
        compiler_params=_cparams("parallel", "arbitrary"),
        name="gated_merge",
    )(ya, yb, yc, wa, wb, wc, proj, proj, proj)


def _hgrn_tables(chunk):
    levels = int(math.log2(chunk))
    assert 1 << levels == chunk
    t = np.arange(chunk)[:, None]
    u = np.arange(chunk)[None, :]
    blocks = [(u <= t), (u > t)]
    masks = [(t == u)]
    for l in range(levels):
        m = 1 << l
        parent_t, parent_u = t // (2 * m), u // (2 * m)
        boundary = parent_t * 2 * m + m - 1
        lower_t = (t % (2 * m)) >= m
        in_lower = lower_t & (u > boundary) & (u <= t)
        in_upper = (~lower_t) & (u > t) & (u <= boundary)
        blocks.append(in_lower | in_upper)
        lower_row = (t % (2 * m)) >= m
        upper_col = (u % (2 * m)) < m
        masks.append(lower_row & upper_col & (parent_t == parent_u))
    sums = np.concatenate([b.astype(np.float32) for b in blocks], axis=0)
    return sums, np.stack([m.astype(np.float32) for m in masks], axis=0)


def _split3(x):
    p1 = x.astype(BF16)
    r1 = x - p1.astype(F32)
    p2 = r1.astype(BF16)
    p3 = (r1 - p2.astype(F32)).astype(BF16)
    return p1, p2, p3


def _dot_nt(a, b):
    return lax.dot_general(a, b, (((1,), (1,)), ((), ())), preferred_element_type=F32)


def _hgrn_kernel(q_ref, f_ref, v_ref, g_ref, lb_ref, gain_ref, sums_ref, masks_ref, o_ref, st_ref, *,
                 chunk, n_chunks):
    levels = masks_ref.shape[0] - 1
    dk = q_ref.shape[1]
    st_ref[...] = jnp.zeros_like(st_ref)
    lb = lb_ref[...]
    gain = gain_ref[...]
    sums = sums_ref[...].astype(BF16)

    def body(c, carry):
        rows = pl.ds(pl.multiple_of(c * chunk, chunk), chunk)
        q = q_ref[rows, :] * (HG_DK ** -0.5)
        f = lb + (1.0 - lb) * jax.nn.sigmoid(f_ref[rows, :])
        logf = jnp.log(f)
        kk = 1.0 - f
        v16 = v_ref[rows, :].astype(BF16)
        parts = jnp.concatenate(_split3(logf), axis=1)
        e3 = jnp.dot(sums, parts, preferred_element_type=F32)
        e = e3[:, :dk] + e3[:, dk:2 * dk] + e3[:, 2 * dk:]
        x = jnp.exp(e)
        b_last = e[chunk - 1:chunk, :]
        q_inter = (q * x[:chunk]).astype(BF16)
        k_inter = (kk * x[chunk:2 * chunk]).astype(BF16)
        scores = masks_ref[0] * _dot_nt(q.astype(BF16), kk.astype(BF16))
        for l in range(levels):
            xl = x[(2 + l) * chunk:(3 + l) * chunk]
            scores += masks_ref[1 + l] * _dot_nt((q * xl).astype(BF16), (kk * xl).astype(BF16))
        st = st_ref[...]
        o = jnp.dot(scores.astype(BF16), v16, preferred_element_type=F32)
        o += _dot_nt(q_inter, st.astype(BF16))
        upd = lax.dot_general(v16, k_inter, (((0,), (0,)), ((), ())), preferred_element_type=F32)
        st_ref[...] = st * jnp.exp(b_last) + upd
        o = o * lax.rsqrt(jnp.mean(o * o, axis=-1, keepdims=True) + NORM_EPS) * gain
        gate = g_ref[rows, :]
        o_ref[rows, :] = (o * (gate * jax.nn.sigmoid(gate))).astype(o_ref.dtype)
        return carry

    lax.fori_loop(0, n_chunks, body, 0)


def hgrn2_mixer(proj, lb, out_gain, batch, seq):
    T = proj.shape[0]
    chunk = HG_CHUNK
    sums, masks = _hgrn_tables(chunk)

    def col_spec(off):
        return pl.BlockSpec((seq, HG_DK), lambda b, h: (b, off // HG_DK + h))

    vec_spec = pl.BlockSpec((1, HG_DK), lambda b, h: (0, h))
    return pl.pallas_call(
        functools.partial(_hgrn_kernel, chunk=chunk, n_chunks=seq // chunk),
        grid=(batch, HG_HEADS),
        in_specs=[col_spec(OFF_HQ), col_spec(OFF_HF), col_spec(OFF_HV), col_spec(OFF_HG),
                  vec_spec, vec_spec,
                  pl.BlockSpec(sums.shape, lambda b, h: (0, 0)),
                  pl.BlockSpec(masks.shape, lambda b, h: (0, 0, 0))],
        out_specs=pl.BlockSpec((seq, HG_DV), lambda b, h: (b, h)),
        out_shape=jax.ShapeDtypeStruct((T, HG_WIDTH), BF16),
        scratch_shapes=[pltpu.VMEM((HG_DV, HG_DK), F32)],
        compiler_params=_cparams("parallel", "parallel"),
        name="hgrn2",
    )(proj, proj, proj, proj, lb.reshape(1, -1), out_gain.reshape(1, -1),
      jnp.asarray(sums), jnp.asarray(masks))


def _pool_kernel(u_ref, w_ref, scale_ref, o_ref):
    seq = u_ref.shape[0]
    row = lax.broadcasted_iota(jnp.int32, (seq, POOL_GROUP_DIM), 0)
    for j, window in enumerate(POOL_WINDOWS):
        cols = slice(j * POOL_GROUP_DIM, (j + 1) * POOL_GROUP_DIM)
        u = u_ref[:, cols]
        acc = u
        span = 1
        while span < window:
            shifted = jnp.where(row >= span, pltpu.roll(acc, span, axis=0), 0.0)
            acc = acc + shifted
            span *= 2
        count = jnp.minimum(row + 1, window).astype(F32)
        pooled = acc / count - u
        mixed = jnp.dot(pooled.astype(BF16), w_ref[j], preferred_element_type=F32)
        o_ref[:, cols] = (mixed * scale_ref[:, cols]).astype(o_ref.dtype)


def pool_mixer(proj, w_groups, scale, batch, seq):
    T = proj.shape[0]
    assert all(w & (w - 1) == 0 for w in POOL_WINDOWS)
    return pl.pallas_call(
        _pool_kernel,
        grid=(batch,),
        in_specs=[pl.BlockSpec((seq, POOL_WIDTH), lambda b: (b, OFF_PU // POOL_WIDTH)),
                  pl.BlockSpec(w_groups.shape, lambda b: (0, 0, 0)),
                  pl.BlockSpec((1, POOL_WIDTH), lambda b: (0, 0))],
        out_specs=pl.BlockSpec((seq, POOL_WIDTH), lambda b: (b, 0)),
        out_shape=jax.ShapeDtypeStruct((T, POOL_WIDTH), BF16),
        compiler_params=_cparams("parallel"),
        name="pool",
    )(proj, w_groups, scale.reshape(1, -1))


def _diffattn_kernel(slopes_ref, q_ref, k_ref, v_ref, lam_ref, subln_ref, o_ref, *, bq, bk, lambda_init):
    h = pl.program_id(1)
    qi = pl.program_id(2)
    slope = slopes_ref[h]
    lp = lam_ref[...]
    lam = (jnp.exp(jnp.sum(lp[0:1] * lp[1:2], axis=-1, keepdims=True))
           - jnp.exp(jnp.sum(lp[2:3] * lp[3:4], axis=-1, keepdims=True)) + lambda_init)
    q = q_ref[...] * (DA_HEAD_DIM ** -0.5)
    lane = lax.broadcasted_iota(jnp.int32, q.shape, 1)
    q_maps = [jnp.where(lane < DA_HEAD_DIM, q, 0.0).astype(BF16),
              jnp.where(lane >= DA_HEAD_DIM, q, 0.0).astype(BF16)]
    dist0 = (qi * bq + lax.broadcasted_iota(jnp.int32, (bq, bk), 0)
             - lax.broadcasted_iota(jnp.int32, (bq, bk), 1))

    def body(j, carry):
        rows = pl.ds(pl.multiple_of(j * bk, bk), bk)
        k16 = k_ref[rows, :].astype(BF16)
        v16 = v_ref[rows, :].astype(BF16)
        dist = dist0 - j * bk
        bias = jnp.where(dist >= 0, -slope * dist.astype(F32), -jnp.inf)
        new = []
        for m in range(2):
            m_prev, l_prev, acc_prev = carry[m]
            s = _dot_nt(q_maps[m], k16) + bias
            m_new = jnp.maximum(m_prev, jnp.max(s, axis=-1, keepdims=True))
            alpha = jnp.exp(m_prev - m_new)
            p = jnp.exp(s - m_new)
            l_new = alpha * l_prev + jnp.sum(p, axis=-1, keepdims=True)
            acc_new = alpha * acc_prev + jnp.dot(p.astype(BF16), v16, preferred_element_type=F32)
            new.append((m_new, l_new, acc_new))
        return tuple(new)

    init_one = (jnp.full((bq, 1), -jnp.inf, F32), jnp.zeros((bq, 1), F32), jnp.zeros((bq, DA_VDIM), F32))
    n_blocks = (qi * bq + bq + bk - 1) // bk
    (m0, l0, a0), (m1, l1, a1) = lax.fori_loop(0, n_blocks, body, (init_one, init_one))
    o = a0 / l0 - lam * (a1 / l1)
    o = o * lax.rsqrt(jnp.mean(o * o, axis=-1, keepdims=True) + NORM_EPS) * subln_ref[...]
    o_ref[...] = (o * (1.0 - lambda_init)).astype(o_ref.dtype)


def diff_attention(proj, lam_params, subln, lambda_init, batch, seq, *, bq=256, bk=256):
    T = proj.shape[0]
    nq = seq // bq
    slopes = (2.0 ** (-8.0 * jnp.arange(1, DA_HEADS + 1, dtype=F32) / DA_HEADS)).astype(F32)
    grid_spec = pltpu.PrefetchScalarGridSpec(
        num_scalar_prefetch=1,
        grid=(batch, DA_HEADS, nq),
        in_specs=[pl.BlockSpec((bq, DA_VDIM), lambda b, h, i, s: (b * nq + i, OFF_DQ // DA_VDIM + h)),
                  pl.BlockSpec((seq, DA_VDIM), lambda b, h, i, s: (b, OFF_DK // DA_VDIM + h)),
                  pl.BlockSpec((seq, DA_VDIM), lambda b, h, i, s: (b, OFF_DV // DA_VDIM + h)),
                  pl.BlockSpec(lam_params.shape, lambda b, h, i, s: (0, 0)),
                  pl.BlockSpec((1, DA_VDIM), lambda b, h, i, s: (0, 0))],
        out_specs=pl.BlockSpec((bq, DA_VDIM), lambda b, h, i, s: (b * nq + i, h)),
    )
    return pl.pallas_call(
        functools.partial(_diffattn_kernel, bq=bq, bk=bk, lambda_init=lambda_init),
        grid_spec=grid_spec,
        out_shape=jax.ShapeDtypeStruct((T, DA_WIDTH), BF16),
        compiler_params=_cparams("parallel", "parallel", "arbitrary"),
        name="diffattn",
    )(slopes, proj, proj, proj, lam_params, subln.reshape(1, -1))


def kernel(x, norm_mix_pre, norm_mix_post, norm_ffn_pre, norm_ffn_post, w_in, hgrn_lb_logits, hgrn_out_norm,
           pool_w, pool_scale, diff_lambda, diff_subln, w_up_a, w_up_b, w_up_c, w_out, w_ffn_gate, w_ffn_up,
           w_ffn_down):
    B, S, D = x.shape
    depth = w_in.shape[0]
    T = B * S
    lb_all = jnp.cumsum(jax.nn.softmax(hgrn_lb_logits.astype(F32), axis=0), axis=0)
    lb_all = lb_all - lb_all[0:1]

    x2 = x.reshape(T, D)
    h = prenorm(x2, norm_mix_pre[0])
    for l in range(depth):
        lambda_init = 0.8 - 0.6 * math.exp(-0.3 * l)
        proj = matmul(h, w_in[l].astype(BF16), bm=1024, bn=512, out_dtype=F32, name="in_proj")
        y_a = hgrn2_mixer(proj, lb_all[l], hgrn_out_norm[l], B, S)
        y_b = pool_mixer(proj, pool_w[l].astype(BF16), pool_scale[l], B, S)
        y_c = diff_attention(proj, diff_lambda[l], diff_subln[l], lambda_init, B, S)
        merged = gated_merge(y_a, y_b, y_c, w_up_a[l].astype(BF16), w_up_b[l].astype(BF16),
                             w_up_c[l].astype(BF16), proj, D, bm=1024, bn=512)
        z = matmul(merged, w_out[l].astype(BF16), bm=1024, bn=512, out_dtype=F32, name="out_proj")
        x2, h = postnorm(x2, z, norm_mix_post[l], norm_ffn_pre[l])
        u = swiglu_up(h, w_ffn_gate[l].astype(BF16), w_ffn_up[l].astype(BF16), bm=1024, bn=256)
        ff = matmul(u, w_ffn_down[l].astype(BF16), bm=512, bn=512, out_dtype=F32, name="ffn_down")
        gnext = norm_mix_pre[l + 1] if l + 1 < depth else None
        x2, h = postnorm(x2, ff, norm_ffn_post[l], gnext)
    return x2.reshape(B, S, D)
```

```python
import functools
import math

import jax
import jax.numpy as jnp
import numpy as np
from jax import lax
from jax.experimental import pallas as pl
from jax.experimental.pallas import tpu as pltpu

HG_HEADS = 8
HG_DK = 128
HG_DV = 128
HG_WIDTH = HG_HEADS * HG_DV
HG_CHUNK = 64
POOL_WINDOWS = (2, 4, 8, 16)
POOL_GROUPS = 4
POOL_GROUP_DIM = 256
POOL_WIDTH = POOL_GROUPS * POOL_GROUP_DIM
DA_HEADS = 8
DA_HEAD_DIM = 64
DA_VDIM = 2 * DA_HEAD_DIM
DA_WIDTH = DA_HEADS * DA_VDIM
N_BRANCH = 3
NORM_EPS = 1e-6
LOG2E = 1.4426950408889634

OFF_HQ = 0
OFF_HF = OFF_HQ + HG_HEADS * HG_DK
OFF_HV = OFF_HF + HG_HEADS * HG_DK
OFF_HG = OFF_HV + HG_WIDTH
OFF_PU = OFF_HG + HG_WIDTH
OFF_DQ = OFF_PU + POOL_WIDTH
OFF_DK = OFF_DQ + DA_WIDTH
OFF_DV = OFF_DK + DA_WIDTH
OFF_GATE = OFF_DV + DA_WIDTH

SUBLANES = 8
V7X_VMEM_LIMIT_BYTES = 56 * 1024 * 1024

BF16 = jnp.bfloat16
F32 = jnp.float32


def _cparams(*sem):
    return pltpu.CompilerParams(dimension_semantics=sem, vmem_limit_bytes=V7X_VMEM_LIMIT_BYTES)


def _rms(x, gain):
    return x * lax.rsqrt(jnp.mean(x * x, axis=-1, keepdims=True) + NORM_EPS) * gain


def _prenorm_kernel(x_ref, g_ref, h_ref):
    h_ref[...] = _rms(x_ref[...], g_ref[...]).astype(h_ref.dtype)


def prenorm(x2, gain, *, rows=256):
    T, D = x2.shape
    return pl.pallas_call(
        _prenorm_kernel,
        grid=(T // rows,),
        in_specs=[pl.BlockSpec((rows, D), lambda i: (i, 0)),
                  pl.BlockSpec((1, D), lambda i: (0, 0))],
        out_specs=pl.BlockSpec((rows, D), lambda i: (i, 0)),
        out_shape=jax.ShapeDtypeStruct((T, D), BF16),
        compiler_params=_cparams("parallel"),
        name="prenorm",
    )(x2, gain.reshape(1, D))


def _postnorm_kernel(x_ref, z_ref, gpost_ref, gnext_ref, xo_ref, h_ref):
    xn = x_ref[...] + _rms(z_ref[...], gpost_ref[...])
    xo_ref[...] = xn
    h_ref[...] = _rms(xn, gnext_ref[...]).astype(h_ref.dtype)


def _postnorm_last_kernel(x_ref, z_ref, gpost_ref, xo_ref):
    xo_ref[...] = x_ref[...] + _rms(z_ref[...], gpost_ref[...])


def postnorm(x2, z, gpost, gnext, *, rows=256):
    T, D = x2.shape
    row_spec = pl.BlockSpec((rows, D), lambda i: (i, 0))
    vec_spec = pl.BlockSpec((1, D), lambda i: (0, 0))
    if gnext is None:
        return pl.pallas_call(
            _postnorm_last_kernel,
            grid=(T // rows,),
            in_specs=[row_spec, row_spec, vec_spec],
            out_specs=row_spec,
            out_shape=jax.ShapeDtypeStruct((T, D), F32),
            compiler_params=_cparams("parallel"),
            name="postnorm_last",
        )(x2, z, gpost.reshape(1, D)), None
    return pl.pallas_call(
        _postnorm_kernel,
        grid=(T // rows,),
        in_specs=[row_spec, row_spec, vec_spec, vec_spec],
        out_specs=[row_spec, row_spec],
        out_shape=[jax.ShapeDtypeStruct((T, D), F32), jax.ShapeDtypeStruct((T, D), BF16)],
        compiler_params=_cparams("parallel"),
        name="postnorm",
    )(x2, z, gpost.reshape(1, D), gnext.reshape(1, D))


def _first_row_tile():
    return pl.program_id(1) == 0


def _mm_ws_kernel(a_ref, w_ref, o_ref, w16_ref):
    @pl.when(_first_row_tile())
    def _():
        w16_ref[...] = w_ref[...].astype(BF16)

    o_ref[...] = jnp.dot(a_ref[...], w16_ref[...], preferred_element_type=F32).astype(o_ref.dtype)


def matmul_ws(a, w_all, layer, *, bm, bn, out_dtype, name):
    M, K = a.shape
    N = w_all.shape[2]
    return pl.pallas_call(
        _mm_ws_kernel,
        grid=(N // bn, M // bm),
        in_specs=[pl.BlockSpec((bm, K), lambda j, i: (i, 0)),
                  pl.BlockSpec((None, K, bn), lambda j, i: (layer, 0, j))],
        out_specs=pl.BlockSpec((bm, bn), lambda j, i: (i, j)),
        out_shape=jax.ShapeDtypeStruct((M, N), out_dtype),
        scratch_shapes=[pltpu.VMEM((K, bn), BF16)],
        compiler_params=_cparams("arbitrary", "arbitrary"),
        name=name,
    )(a, w_all)


def _mm_kernel(a_ref, w_ref, o_ref):
    o_ref[...] = jnp.dot(a_ref[...], w_ref[...], preferred_element_type=F32).astype(o_ref.dtype)


def matmul_bf16(a, w, *, bm, bn, out_dtype, name):
    M, K = a.shape
    N = w.shape[1]
    return pl.pallas_call(
        _mm_kernel,
        grid=(M // bm, N // bn),
        in_specs=[pl.BlockSpec((bm, K), lambda i, j: (i, 0)),
                  pl.BlockSpec((K, bn), lambda i, j: (0, j))],
        out_specs=pl.BlockSpec((bm, bn), lambda i, j: (i, j)),
        out_shape=jax.ShapeDtypeStruct((M, N), out_dtype),
        compiler_params=_cparams("parallel", "arbitrary"),
        name=name,
    )(a, w)


def _swiglu_kernel(h_ref, wg_ref, wu_ref, wd_ref, o_ref, wd16_ref, wg16_ref, wu16_ref):
    @pl.when(_first_row_tile())
    def _():
        wg16_ref[...] = wg_ref[...].astype(BF16)
        wu16_ref[...] = wu_ref[...].astype(BF16)
        wd16_ref[...] = wd_ref[...].astype(BF16)

    h = h_ref[...]
    a = jnp.dot(h, wg16_ref[...], preferred_element_type=F32)
    b = jnp.dot(h, wu16_ref[...], preferred_element_type=F32)
    o_ref[...] = (a * jax.nn.sigmoid(a) * b).astype(o_ref.dtype)


def swiglu_up(h, wg_all, wu_all, wd_all, layer, *, bm, bn):
    M, K = h.shape
    F = wg_all.shape[2]
    D = wd_all.shape[2]
    w_spec = pl.BlockSpec((None, K, bn), lambda j, i: (layer, 0, j))
    return pl.pallas_call(
        _swiglu_kernel,
        grid=(F // bn, M // bm),
        in_specs=[pl.BlockSpec((bm, K), lambda j, i: (i, 0)), w_spec, w_spec,
                  pl.BlockSpec((None, bn, D), lambda j, i: (layer, j, 0))],
        out_specs=[pl.BlockSpec((bm, bn), lambda j, i: (i, j)),
                   pl.BlockSpec((bn, D), lambda j, i: (j, 0))],
        out_shape=[jax.ShapeDtypeStruct((M, F), BF16), jax.ShapeDtypeStruct((F, D), BF16)],
        scratch_shapes=[pltpu.VMEM((K, bn), BF16), pltpu.VMEM((K, bn), BF16)],
        compiler_params=_cparams("arbitrary", "arbitrary"),
        name="swiglu_up",
    )(h, wg_all, wu_all, wd_all)


def _merge_kernel(ya_ref, yb_ref, yc_ref, wa_ref, wb_ref, wc_ref, ga_ref, gb_ref, gc_ref, o_ref,
                  wa16_ref, wb16_ref, wc16_ref):
    @pl.when(_first_row_tile())
    def _():
        wa16_ref[...] = wa_ref[...].astype(BF16)
        wb16_ref[...] = wb_ref[...].astype(BF16)
        wc16_ref[...] = wc_ref[...].astype(BF16)

    acc = jax.nn.sigmoid(ga_ref[...]) * jnp.dot(ya_ref[...], wa16_ref[...], preferred_element_type=F32)
    acc += jax.nn.sigmoid(gb_ref[...]) * jnp.dot(yb_ref[...], wb16_ref[...], preferred_element_type=F32)
    acc += jax.nn.sigmoid(gc_ref[...]) * jnp.dot(yc_ref[...], wc16_ref[...], preferred_element_type=F32)
    o_ref[...] = acc.astype(o_ref.dtype)


def gated_merge(ya, yb, yc, wa_all, wb_all, wc_all, layer, proj, d_model, *, bm, bn):
    M, Kb = ya.shape
    gate_blk = OFF_GATE // bn
    per_branch = d_model // bn
    y_spec = pl.BlockSpec((bm, Kb), lambda j, i: (i, 0))
    w_spec = pl.BlockSpec((None, Kb, bn), lambda j, i: (layer, 0, j))

    def g_spec(k):
        return pl.BlockSpec((bm, bn), lambda j, i: (i, gate_blk + k * per_branch + j))

    return pl.pallas_call(
        _merge_kernel,
        grid=(d_model // bn, M // bm),
        in_specs=[y_spec, y_spec, y_spec, w_spec, w_spec, w_spec, g_spec(0), g_spec(1), g_spec(2)],
        out_specs=pl.BlockSpec((bm, bn), lambda j, i: (i, j)),
        out_shape=jax.ShapeDtypeStruct((M, d_model), BF16),
        scratch_shapes=[pltpu.VMEM((Kb, bn), BF16)] * 3,
        compiler_params=_cparams("arbitrary", "arbitrary"),
        name="gated_merge",
    )(ya, yb, yc, wa_all, wb_all, wc_all, proj, proj, proj)


def _hgrn_tables(chunk):
    levels = int(math.log2(chunk))
    assert 1 << levels == chunk
    t = np.arange(chunk)[:, None]
    u = np.arange(chunk)[None, :]
    blocks = [(u <= t), (u > t)]
    masks = [(t == u)]
    for l in range(levels):
        m = 1 << l
        parent_t, parent_u = t // (2 * m), u // (2 * m)
        boundary = parent_t * 2 * m + m - 1
        lower_t = (t % (2 * m)) >= m
        in_lower = lower_t & (u > boundary) & (u <= t)
        in_upper = (~lower_t) & (u > t) & (u <= boundary)
        blocks.append(in_lower | in_upper)
        upper_col = (u % (2 * m)) < m
        masks.append(lower_t & upper_col & (parent_t == parent_u))
    sums = np.concatenate([b.astype(np.float32) for b in blocks], axis=0)
    return sums, np.stack([m.astype(np.float32) for m in masks], axis=0)


def _split3_f32(x):
    p1 = x.astype(BF16).astype(F32)
    r1 = x - p1
    p2 = r1.astype(BF16).astype(F32)
    return p1, p2, r1 - p2


def _dot_nt(a, b):
    return lax.dot_general(a, b, (((1,), (1,)), ((), ())), preferred_element_type=F32)


def _hgrn_kernel(q_ref, f_ref, v_ref, g_ref, lb_ref, gain_ref, sums_ref, masks_ref, o_ref, st_ref, *,
                 chunk, n_chunks, unroll):
    levels = masks_ref.shape[0] - 1
    dk = q_ref.shape[1]
    st_ref[...] = jnp.zeros_like(st_ref)
    lb = lb_ref[...]
    gain = gain_ref[...]
    sums = sums_ref[...].astype(BF16)

    def body(c, carry):
        rows = pl.ds(pl.multiple_of(c * chunk, chunk), chunk)
        q = q_ref[rows, :] * (HG_DK ** -0.5)
        f = lb + (1.0 - lb) * jax.nn.sigmoid(f_ref[rows, :])
        logf = jnp.log(f)
        kk = 1.0 - f
        v16 = v_ref[rows, :].astype(BF16)
        parts = jnp.concatenate([p.astype(BF16) for p in _split3_f32(logf)], axis=1)
        e3 = jnp.dot(sums, parts, preferred_element_type=F32)
        e = e3[:, :dk] + e3[:, dk:2 * dk] + e3[:, 2 * dk:]
        x = jnp.exp(e)
        b_last = e[chunk - 1:chunk, :]
        q_inter = (q * x[:chunk]).astype(BF16)
        k_inter = (kk * x[chunk:2 * chunk]).astype(BF16)
        scores = masks_ref[0] * _dot_nt(q.astype(BF16), kk.astype(BF16))
        for l in range(levels):
            xl = x[(2 + l) * chunk:(3 + l) * chunk]
            scores += masks_ref[1 + l] * _dot_nt((q * xl).astype(BF16), (kk * xl).astype(BF16))
        st = st_ref[...]
        o = jnp.dot(scores.astype(BF16), v16, preferred_element_type=F32)
        o += _dot_nt(q_inter, st.astype(BF16))
        upd = lax.dot_general(v16, k_inter, (((0,), (0,)), ((), ())), preferred_element_type=F32)
        st_ref[...] = st * jnp.exp(b_last) + upd
        o = o * lax.rsqrt(jnp.mean(o * o, axis=-1, keepdims=True) + NORM_EPS) * gain
        gate = g_ref[rows, :]
        o_ref[rows, :] = (o * (gate * jax.nn.sigmoid(gate))).astype(o_ref.dtype)
        return carry

    lax.fori_loop(0, n_chunks, body, 0, unroll=unroll)


def hgrn2_mixer(proj, lb, out_gain, batch, seq, *, chunk=HG_CHUNK, unroll=4):
    T = proj.shape[0]
    sums, masks = _hgrn_tables(chunk)

    def col_spec(off):
        return pl.BlockSpec((seq, HG_DK), lambda b, h: (b, off // HG_DK + h))

    vec_spec = pl.BlockSpec((1, HG_DK), lambda b, h: (0, h))
    return pl.pallas_call(
        functools.partial(_hgrn_kernel, chunk=chunk, n_chunks=seq // chunk, unroll=unroll),
        grid=(batch, HG_HEADS),
        in_specs=[col_spec(OFF_HQ), col_spec(OFF_HF), col_spec(OFF_HV), col_spec(OFF_HG),
                  vec_spec, vec_spec,
                  pl.BlockSpec(sums.shape, lambda b, h: (0, 0)),
                  pl.BlockSpec(masks.shape, lambda b, h: (0, 0, 0))],
        out_specs=pl.BlockSpec((seq, HG_DV), lambda b, h: (b, h)),
        out_shape=jax.ShapeDtypeStruct((T, HG_WIDTH), BF16),
        scratch_shapes=[pltpu.VMEM((HG_DV, HG_DK), F32)],
        compiler_params=_cparams("parallel", "parallel"),
        name="hgrn2",
    )(proj, proj, proj, proj, lb.reshape(1, -1), out_gain.reshape(1, -1),
      jnp.asarray(sums), jnp.asarray(masks))


def _pool_kernel(u_ref, w_ref, scale_ref, o_ref):
    seq = u_ref.shape[0]
    row = lax.broadcasted_iota(jnp.int32, (seq, POOL_GROUP_DIM), 0)
    for j, window in enumerate(POOL_WINDOWS):
        cols = slice(j * POOL_GROUP_DIM, (j + 1) * POOL_GROUP_DIM)
        u = u_ref[:, cols]
        acc = u
        span = 1
        while span < window:
            shifted = jnp.where(row >= span, pltpu.roll(acc, span, axis=0), 0.0)
            acc = acc + shifted
            span *= 2
        count = jnp.minimum(row + 1, window).astype(F32)
        pooled = acc / count - u
        mixed = jnp.dot(pooled.astype(BF16), w_ref[j].astype(BF16), preferred_element_type=F32)
        o_ref[:, cols] = (mixed * scale_ref[:, cols]).astype(o_ref.dtype)


def pool_mixer(proj, w_groups, scale, batch, seq):
    T = proj.shape[0]
    assert all(w & (w - 1) == 0 for w in POOL_WINDOWS)
    return pl.pallas_call(
        _pool_kernel,
        grid=(batch,),
        in_specs=[pl.BlockSpec((seq, POOL_WIDTH), lambda b: (b, OFF_PU // POOL_WIDTH)),
                  pl.BlockSpec(w_groups.shape, lambda b: (0, 0, 0)),
                  pl.BlockSpec((1, POOL_WIDTH), lambda b: (0, 0))],
        out_specs=pl.BlockSpec((seq, POOL_WIDTH), lambda b: (b, 0)),
        out_shape=jax.ShapeDtypeStruct((T, POOL_WIDTH), BF16),
        compiler_params=_cparams("parallel"),
        name="pool",
    )(proj, w_groups, scale.reshape(1, -1))


DA_AUG_ROWS = DA_VDIM + SUBLANES


def _diffattn_kernel(slopes_ref, q_ref, k_ref, v_ref, lam_ref, subln_ref, o_ref, k0_ref, k1_ref, vt_ref, *,
                     seq, blk, lambda_init):
    slope = slopes_ref[pl.program_id(1)]
    lp = lam_ref[...]
    lam = (jnp.exp(jnp.sum(lp[0:1] * lp[1:2], axis=-1, keepdims=True))
           - jnp.exp(jnp.sum(lp[2:3] * lp[3:4], axis=-1, keepdims=True)) + lambda_init)

    lane = lax.broadcasted_iota(jnp.int32, (seq, DA_VDIM), 1)
    kpos = lax.broadcasted_iota(jnp.int32, (seq, DA_VDIM), 0).astype(F32) * (slope * LOG2E)
    p1, p2, p3 = _split3_f32(kpos)
    k = k_ref[...]

    def with_bias(base_lane, keep):
        extra = jnp.where(lane == base_lane, p1,
                          jnp.where(lane == base_lane + 1, p2, jnp.where(lane == base_lane + 2, p3, 0.0)))
        return jnp.where(keep, k, extra).astype(BF16)

    k0_ref[...] = with_bias(DA_HEAD_DIM, lane < DA_HEAD_DIM)
    k1_ref[...] = with_bias(0, lane >= DA_HEAD_DIM)
    vt_ref[0:DA_VDIM, :] = v_ref[...].T.astype(BF16)
    aug_row = lax.broadcasted_iota(jnp.int32, (SUBLANES, seq), 0)
    vt_ref[DA_VDIM:DA_AUG_ROWS, :] = jnp.where(aug_row == 0, 1.0, 0.0).astype(BF16)

    qlane = lax.broadcasted_iota(jnp.int32, (blk, DA_VDIM), 1)
    key_gt_query = (lax.broadcasted_iota(jnp.int32, (blk, blk), 0)
                    > lax.broadcasted_iota(jnp.int32, (blk, blk), 1))

    def step(k_ref_m, q_m, kstart, width, diagonal, m_prev, acc_prev):
        keys = pl.ds(kstart, width)
        s = _dot_nt(k_ref_m[keys, :], q_m)
        if diagonal:
            s = jnp.where(key_gt_query, -jnp.inf, s)
        m_new = jnp.maximum(m_prev, jnp.max(s, axis=0, keepdims=True))
        alpha = jnp.exp2(m_prev - m_new)
        p = jnp.exp2(s - m_new).astype(BF16)
        acc_new = alpha * acc_prev + jnp.dot(vt_ref[:, keys], p, preferred_element_type=F32)
        return m_new, acc_new

    for i in range(seq // blk):
        rows = pl.ds(i * blk, blk)
        q = q_ref[rows, :] * (DA_HEAD_DIM ** -0.5 * LOG2E)
        one0 = (qlane >= DA_HEAD_DIM) & (qlane < DA_HEAD_DIM + 3)
        one1 = qlane < 3
        q_maps = [jnp.where(qlane < DA_HEAD_DIM, q, jnp.where(one0, 1.0, 0.0)).astype(BF16),
                  jnp.where(qlane >= DA_HEAD_DIM, q, jnp.where(one1, 1.0, 0.0)).astype(BF16)]
        outs = []
        for k_ref_m, q_m in zip((k0_ref, k1_ref), q_maps):
            m_run = jnp.full((1, blk), -jnp.inf, F32)
            acc = jnp.zeros((DA_AUG_ROWS, blk), F32)
            start = 0
            while start < i * blk:
                width = min(2 * blk, i * blk - start)
                m_run, acc = step(k_ref_m, q_m, start, width, False, m_run, acc)
                start += width
            m_run, acc = step(k_ref_m, q_m, i * blk, blk, True, m_run, acc)
            outs.append(acc[0:DA_VDIM] * (1.0 / acc[DA_VDIM:DA_VDIM + 1]))
        o = (outs[0] - lam * outs[1]).T
        o = o * lax.rsqrt(jnp.mean(o * o, axis=-1, keepdims=True) + NORM_EPS) * subln_ref[...]
        o_ref[rows, :] = (o * (1.0 - lambda_init)).astype(o_ref.dtype)


def diff_attention(proj, lam_params, subln, lambda_init, batch, seq, *, blk=256):
    T = proj.shape[0]
    slopes = (2.0 ** (-8.0 * jnp.arange(1, DA_HEADS + 1, dtype=F32) / DA_HEADS)).astype(F32)

    def col_spec(off):
        return pl.BlockSpec((seq, DA_VDIM), lambda b, h, s: (b, off // DA_VDIM + h))

    grid_spec = pltpu.PrefetchScalarGridSpec(
        num_scalar_prefetch=1,
        grid=(batch, DA_HEADS),
        in_specs=[col_spec(OFF_DQ), col_spec(OFF_DK), col_spec(OFF_DV),
                  pl.BlockSpec(lam_params.shape, lambda b, h, s: (0, 0)),
                  pl.BlockSpec((1, DA_VDIM), lambda b, h, s: (0, 0))],
        out_specs=pl.BlockSpec((seq, DA_VDIM), lambda b, h, s: (b, h)),
        scratch_shapes=[pltpu.VMEM((seq, DA_VDIM), BF16), pltpu.VMEM((seq, DA_VDIM), BF16),
                        pltpu.VMEM((DA_AUG_ROWS, seq), BF16)],
    )
    return pl.pallas_call(
        functools.partial(_diffattn_kernel, seq=seq, blk=blk, lambda_init=lambda_init),
        grid_spec=grid_spec,
        out_shape=jax.ShapeDtypeStruct((T, DA_WIDTH), BF16),
        compiler_params=_cparams("parallel", "parallel"),
        name="diffattn",
    )(slopes, proj, proj, proj, lam_params, subln.reshape(1, -1))


def kernel(x, norm_mix_pre, norm_mix_post, norm_ffn_pre, norm_ffn_post, w_in, hgrn_lb_logits, hgrn_out_norm,
           pool_w, pool_scale, diff_lambda, diff_subln, w_up_a, w_up_b, w_up_c, w_out, w_ffn_gate, w_ffn_up,
           w_ffn_down):
    B, S, D = x.shape
    depth = w_in.shape[0]
    T = B * S
    lb_all = jnp.cumsum(jax.nn.softmax(hgrn_lb_logits.astype(F32), axis=0), axis=0)
    lb_all = lb_all - lb_all[0:1]

    x2 = x.reshape(T, D)
    h = prenorm(x2, norm_mix_pre[0])
    for l in range(depth):
        lambda_init = 0.8 - 0.6 * math.exp(-0.3 * l)
        proj = matmul_ws(h, w_in, l, bm=1024, bn=512, out_dtype=F32, name="in_proj")
        y_a = hgrn2_mixer(proj, lb_all[l], hgrn_out_norm[l], B, S)
        y_b = pool_mixer(proj, pool_w[l], pool_scale[l], B, S)
        y_c = diff_attention(proj, diff_lambda[l], diff_subln[l], lambda_init, B, S)
        merged = gated_merge(y_a, y_b, y_c, w_up_a, w_up_b, w_up_c, l, proj, D, bm=1024, bn=512)
        z = matmul_ws(merged, w_out, l, bm=1024, bn=512, out_dtype=F32, name="out_proj")
        x2, h = postnorm(x2, z, norm_mix_post[l], norm_ffn_pre[l])
        u, wd16 = swiglu_up(h, w_ffn_gate, w_ffn_up, w_ffn_down, l, bm=1024, bn=256)
        ff = matmul_bf16(u, wd16, bm=512, bn=512, out_dtype=F32, name="ffn_down")
        gnext = norm_mix_pre[l + 1] if l + 1 < depth else None
        x2, h = postnorm(x2, ff, norm_ffn_post[l], gnext)
    return x2.reshape(B, S, D)
```

```python
import functools
import math

import jax
import jax.numpy as jnp
import numpy as np
from jax import lax
from jax.experimental import pallas as pl
from jax.experimental.pallas import tpu as pltpu

HG_HEADS = 8
HG_DK = 128
HG_DV = 128
HG_WIDTH = HG_HEADS * HG_DV
HG_CHUNK = 64
POOL_WINDOWS = (2, 4, 8, 16)
POOL_GROUPS = 4
POOL_GROUP_DIM = 256
POOL_WIDTH = POOL_GROUPS * POOL_GROUP_DIM
DA_HEADS = 8
DA_HEAD_DIM = 64
DA_VDIM = 2 * DA_HEAD_DIM
DA_WIDTH = DA_HEADS * DA_VDIM
N_BRANCH = 3
NORM_EPS = 1e-6
LOG2E = 1.4426950408889634

OFF_HQ = 0
OFF_HF = OFF_HQ + HG_HEADS * HG_DK
OFF_HV = OFF_HF + HG_HEADS * HG_DK
OFF_HG = OFF_HV + HG_WIDTH
OFF_PU = OFF_HG + HG_WIDTH
OFF_DQ = OFF_PU + POOL_WIDTH
OFF_DK = OFF_DQ + DA_WIDTH
OFF_DV = OFF_DK + DA_WIDTH
OFF_GATE = OFF_DV + DA_WIDTH

SUBLANES = 8
V7X_VMEM_LIMIT_BYTES = 56 * 1024 * 1024

BF16 = jnp.bfloat16
F32 = jnp.float32


def _cparams(*sem):
    return pltpu.CompilerParams(dimension_semantics=sem, vmem_limit_bytes=V7X_VMEM_LIMIT_BYTES)


def _rms(x, gain):
    return x * lax.rsqrt(jnp.mean(x * x, axis=-1, keepdims=True) + NORM_EPS) * gain


def _prenorm_kernel(x_ref, g_ref, h_ref):
    h_ref[...] = _rms(x_ref[...], g_ref[...]).astype(h_ref.dtype)


def prenorm(x2, gain, *, rows=256):
    T, D = x2.shape
    return pl.pallas_call(
        _prenorm_kernel,
        grid=(T // rows,),
        in_specs=[pl.BlockSpec((rows, D), lambda i: (i, 0)),
                  pl.BlockSpec((1, D), lambda i: (0, 0))],
        out_specs=pl.BlockSpec((rows, D), lambda i: (i, 0)),
        out_shape=jax.ShapeDtypeStruct((T, D), BF16),
        compiler_params=_cparams("parallel"),
        name="prenorm",
    )(x2, gain.reshape(1, D))


def _postnorm_kernel(x_ref, z_ref, gpost_ref, gnext_ref, xo_ref, h_ref):
    xn = x_ref[...] + _rms(z_ref[...], gpost_ref[...])
    xo_ref[...] = xn
    h_ref[...] = _rms(xn, gnext_ref[...]).astype(h_ref.dtype)


def _postnorm_last_kernel(x_ref, z_ref, gpost_ref, xo_ref):
    xo_ref[...] = x_ref[...] + _rms(z_ref[...], gpost_ref[...])


def postnorm(x2, z, gpost, gnext, *, rows=256):
    T, D = x2.shape
    row_spec = pl.BlockSpec((rows, D), lambda i: (i, 0))
    vec_spec = pl.BlockSpec((1, D), lambda i: (0, 0))
    if gnext is None:
        return pl.pallas_call(
            _postnorm_last_kernel,
            grid=(T // rows,),
            in_specs=[row_spec, row_spec, vec_spec],
            out_specs=row_spec,
            out_shape=jax.ShapeDtypeStruct((T, D), F32),
            compiler_params=_cparams("parallel"),
            name="postnorm_last",
        )(x2, z, gpost.reshape(1, D)), None
    return pl.pallas_call(
        _postnorm_kernel,
        grid=(T // rows,),
        in_specs=[row_spec, row_spec, vec_spec, vec_spec],
        out_specs=[row_spec, row_spec],
        out_shape=[jax.ShapeDtypeStruct((T, D), F32), jax.ShapeDtypeStruct((T, D), BF16)],
        compiler_params=_cparams("parallel"),
        name="postnorm",
    )(x2, z, gpost.reshape(1, D), gnext.reshape(1, D))


def _first_row_tile():
    return pl.program_id(1) == 0


def _mm_ws_kernel(a_ref, w_ref, o_ref, w16_ref):
    @pl.when(_first_row_tile())
    def _():
        w16_ref[...] = w_ref[...].astype(BF16)

    o_ref[...] = jnp.dot(a_ref[...], w16_ref[...], preferred_element_type=F32).astype(o_ref.dtype)


def matmul_ws(a, w_all, layer, *, bm, bn, out_dtype, name):
    M, K = a.shape
    N = w_all.shape[2]
    return pl.pallas_call(
        _mm_ws_kernel,
        grid=(N // bn, M // bm),
        in_specs=[pl.BlockSpec((bm, K), lambda j, i: (i, 0)),
                  pl.BlockSpec((None, K, bn), lambda j, i: (layer, 0, j))],
        out_specs=pl.BlockSpec((bm, bn), lambda j, i: (i, j)),
        out_shape=jax.ShapeDtypeStruct((M, N), out_dtype),
        scratch_shapes=[pltpu.VMEM((K, bn), BF16)],
        compiler_params=_cparams("arbitrary", "arbitrary"),
        name=name,
    )(a, w_all)


def _mm_kernel(a_ref, w_ref, o_ref):
    o_ref[...] = jnp.dot(a_ref[...], w_ref[...], preferred_element_type=F32).astype(o_ref.dtype)


def matmul_bf16(a, w, *, bm, bn, out_dtype, name):
    M, K = a.shape
    N = w.shape[1]
    return pl.pallas_call(
        _mm_kernel,
        grid=(M // bm, N // bn),
        in_specs=[pl.BlockSpec((bm, K), lambda i, j: (i, 0)),
                  pl.BlockSpec((K, bn), lambda i, j: (0, j))],
        out_specs=pl.BlockSpec((bm, bn), lambda i, j: (i, j)),
        out_shape=jax.ShapeDtypeStruct((M, N), out_dtype),
        compiler_params=_cparams("parallel", "arbitrary"),
        name=name,
    )(a, w)


def _swiglu_kernel(h_ref, wg_ref, wu_ref, wd_ref, o_ref, wd16_ref, wg16_ref, wu16_ref):
    @pl.when(_first_row_tile())
    def _():
        wg16_ref[...] = wg_ref[...].astype(BF16)
        wu16_ref[...] = wu_ref[...].astype(BF16)
        wd16_ref[...] = wd_ref[...].astype(BF16)

    h = h_ref[...]
    a = jnp.dot(h, wg16_ref[...], preferred_element_type=F32)
    b = jnp.dot(h, wu16_ref[...], preferred_element_type=F32)
    o_ref[...] = (a * jax.nn.sigmoid(a) * b).astype(o_ref.dtype)


def swiglu_up(h, wg_all, wu_all, wd_all, layer, *, bm, bn):
    M, K = h.shape
    F = wg_all.shape[2]
    D = wd_all.shape[2]
    w_spec = pl.BlockSpec((None, K, bn), lambda j, i: (layer, 0, j))
    return pl.pallas_call(
        _swiglu_kernel,
        grid=(F // bn, M // bm),
        in_specs=[pl.BlockSpec((bm, K), lambda j, i: (i, 0)), w_spec, w_spec,
                  pl.BlockSpec((None, bn, D), lambda j, i: (layer, j, 0))],
        out_specs=[pl.BlockSpec((bm, bn), lambda j, i: (i, j)),
                   pl.BlockSpec((bn, D), lambda j, i: (j, 0))],
        out_shape=[jax.ShapeDtypeStruct((M, F), BF16), jax.ShapeDtypeStruct((F, D), BF16)],
        scratch_shapes=[pltpu.VMEM((K, bn), BF16), pltpu.VMEM((K, bn), BF16)],
        compiler_params=_cparams("arbitrary", "arbitrary"),
        name="swiglu_up",
    )(h, wg_all, wu_all, wd_all)


def _merge_kernel(ya_ref, yb_ref, yc_ref, wa_ref, wb_ref, wc_ref, ga_ref, gb_ref, gc_ref, o_ref,
                  wa16_ref, wb16_ref, wc16_ref):
    @pl.when(_first_row_tile())
    def _():
        wa16_ref[...] = wa_ref[...].astype(BF16)
        wb16_ref[...] = wb_ref[...].astype(BF16)
        wc16_ref[...] = wc_ref[...].astype(BF16)

    acc = jax.nn.sigmoid(ga_ref[...]) * jnp.dot(ya_ref[...], wa16_ref[...], preferred_element_type=F32)
    acc += jax.nn.sigmoid(gb_ref[...]) * jnp.dot(yb_ref[...], wb16_ref[...], preferred_element_type=F32)
    acc += jax.nn.sigmoid(gc_ref[...]) * jnp.dot(yc_ref[...], wc16_ref[...], preferred_element_type=F32)
    o_ref[...] = acc.astype(o_ref.dtype)


def gated_merge(ya, yb, yc, wa_all, wb_all, wc_all, layer, proj, d_model, *, bm, bn):
    M, Kb = ya.shape
    gate_blk = OFF_GATE // bn
    per_branch = d_model // bn
    y_spec = pl.BlockSpec((bm, Kb), lambda j, i: (i, 0))
    w_spec = pl.BlockSpec((None, Kb, bn), lambda j, i: (layer, 0, j))

    def g_spec(k):
        return pl.BlockSpec((bm, bn), lambda j, i: (i, gate_blk + k * per_branch + j))

    return pl.pallas_call(
        _merge_kernel,
        grid=(d_model // bn, M // bm),
        in_specs=[y_spec, y_spec, y_spec, w_spec, w_spec, w_spec, g_spec(0), g_spec(1), g_spec(2)],
        out_specs=pl.BlockSpec((bm, bn), lambda j, i: (i, j)),
        out_shape=jax.ShapeDtypeStruct((M, d_model), BF16),
        scratch_shapes=[pltpu.VMEM((Kb, bn), BF16)] * 3,
        compiler_params=_cparams("arbitrary", "arbitrary"),
        name="gated_merge",
    )(ya, yb, yc, wa_all, wb_all, wc_all, proj, proj, proj)


def _hgrn_tables(chunk):
    levels = int(math.log2(chunk))
    assert 1 << levels == chunk and levels >= 3
    t = np.arange(chunk)[:, None]
    u = np.arange(chunk)[None, :]
    masks = [(t == u)]
    for l in range(levels):
        m = 1 << l
        lower_t = (t % (2 * m)) >= m
        upper_u = (u % (2 * m)) < m
        masks.append(lower_t & upper_u & (t // (2 * m) == u // (2 * m)))
    return (u <= t).astype(np.float32), np.stack([m.astype(np.float32) for m in masks], axis=0)


def _split3_f32(x):
    p1 = x.astype(BF16).astype(F32)
    r1 = x - p1
    p2 = r1.astype(BF16).astype(F32)
    return p1, p2, r1 - p2


def _dot_nt(a, b):
    return lax.dot_general(a, b, (((1,), (1,)), ((), ())), preferred_element_type=F32)


def _hgrn_level_decays(f, b, chunk):
    dk = f.shape[1]
    row = lax.broadcasted_iota(jnp.int32, f.shape, 0)
    f_prev = pltpu.roll(f, 1, axis=0)
    f_next = pltpu.roll(f, chunk - 1, axis=0)
    r4 = row % 4
    decays = [jnp.where(row % 2 == 1, f, 1.0),
              jnp.where(r4 == 0, f_next, jnp.where(r4 == 1, 1.0, jnp.where(r4 == 2, f, f * f_prev)))]
    m = 4
    while m < chunk:
        ref = jnp.concatenate(
            [jnp.broadcast_to(b[p * 2 * m + m - 1:p * 2 * m + m, :], (2 * m, dk)) for p in range(chunk // (2 * m))],
            axis=0)
        lower = (row % (2 * m)) >= m
        decays.append(jnp.exp(jnp.where(lower, b - ref, ref - b)))
        m *= 2
    return decays


def _hgrn_kernel(q_ref, f_ref, v_ref, g_ref, lb_ref, gain_ref, tri_ref, masks_ref, o_ref, *,
                 chunk, n_chunks, unroll):
    dk = q_ref.shape[1]
    lb = lb_ref[...]
    gain = gain_ref[...]
    tri = tri_ref[...].astype(BF16)

    def body(c, st):
        rows = pl.ds(pl.multiple_of(c * chunk, chunk), chunk)
        q = q_ref[rows, :] * (HG_DK ** -0.5)
        f = lb + (1.0 - lb) * jax.nn.sigmoid(f_ref[rows, :])
        kk = 1.0 - f
        v16 = v_ref[rows, :].astype(BF16)
        parts = jnp.concatenate([p.astype(BF16) for p in _split3_f32(jnp.log(f))], axis=1)
        b3 = jnp.dot(tri, parts, preferred_element_type=F32)
        b = b3[:, :dk] + b3[:, dk:2 * dk] + b3[:, 2 * dk:]
        b_last = b[chunk - 1:chunk, :]
        q_inter = (q * jnp.exp(b)).astype(BF16)
        k_inter = (kk * jnp.exp(b_last - b)).astype(BF16)
        scores = masks_ref[0] * _dot_nt(q.astype(BF16), kk.astype(BF16))
        for l, xl in enumerate(_hgrn_level_decays(f, b, chunk)):
            scores += masks_ref[1 + l] * _dot_nt((q * xl).astype(BF16), (kk * xl).astype(BF16))
        o = jnp.dot(scores.astype(BF16), v16, preferred_element_type=F32)
        o += _dot_nt(q_inter, st.astype(BF16))
        upd = lax.dot_general(v16, k_inter, (((0,), (0,)), ((), ())), preferred_element_type=F32)
        o = o * lax.rsqrt(jnp.mean(o * o, axis=-1, keepdims=True) + NORM_EPS) * gain
        gate = g_ref[rows, :]
        o_ref[rows, :] = (o * (gate * jax.nn.sigmoid(gate))).astype(o_ref.dtype)
        return st * jnp.exp(b_last) + upd

    lax.fori_loop(0, n_chunks, body, jnp.zeros((HG_DV, dk), F32), unroll=unroll)


def hgrn2_mixer(proj, lb, out_gain, batch, seq, *, chunk=256, unroll=2):
    T = proj.shape[0]
    tri, masks = _hgrn_tables(chunk)

    def col_spec(off):
        return pl.BlockSpec((seq, HG_DK), lambda b, h: (b, off // HG_DK + h))

    vec_spec = pl.BlockSpec((1, HG_DK), lambda b, h: (0, h))
    return pl.pallas_call(
        functools.partial(_hgrn_kernel, chunk=chunk, n_chunks=seq // chunk, unroll=unroll),
        grid=(batch, HG_HEADS),
        in_specs=[col_spec(OFF_HQ), col_spec(OFF_HF), col_spec(OFF_HV), col_spec(OFF_HG),
                  vec_spec, vec_spec,
                  pl.BlockSpec(tri.shape, lambda b, h: (0, 0)),
                  pl.BlockSpec(masks.shape, lambda b, h: (0, 0, 0))],
        out_specs=pl.BlockSpec((seq, HG_DV), lambda b, h: (b, h)),
        out_shape=jax.ShapeDtypeStruct((T, HG_WIDTH), BF16),
        compiler_params=_cparams("parallel", "parallel"),
        name="hgrn2",
    )(proj, proj, proj, proj, lb.reshape(1, -1), out_gain.reshape(1, -1),
      jnp.asarray(tri), jnp.asarray(masks))


def _pool_kernel(u_ref, w_ref, scale_ref, o_ref):
    seq = u_ref.shape[0]
    row = lax.broadcasted_iota(jnp.int32, (seq, POOL_GROUP_DIM), 0)
    for j, window in enumerate(POOL_WINDOWS):
        cols = slice(j * POOL_GROUP_DIM, (j + 1) * POOL_GROUP_DIM)
        u = u_ref[:, cols]
        acc = u
        span = 1
        while span < window:
            shifted = jnp.where(row >= span, pltpu.roll(acc, span, axis=0), 0.0)
            acc = acc + shifted
            span *= 2
        count = jnp.minimum(row + 1, window).astype(F32)
        pooled = acc / count - u
        mixed = jnp.dot(pooled.astype(BF16), w_ref[j].astype(BF16), preferred_element_type=F32)
        o_ref[:, cols] = (mixed * scale_ref[:, cols]).astype(o_ref.dtype)


def pool_mixer(proj, w_groups, scale, batch, seq):
    T = proj.shape[0]
    assert all(w & (w - 1) == 0 for w in POOL_WINDOWS)
    return pl.pallas_call(
        _pool_kernel,
        grid=(batch,),
        in_specs=[pl.BlockSpec((seq, POOL_WIDTH), lambda b: (b, OFF_PU // POOL_WIDTH)),
                  pl.BlockSpec(w_groups.shape, lambda b: (0, 0, 0)),
                  pl.BlockSpec((1, POOL_WIDTH), lambda b: (0, 0))],
        out_specs=pl.BlockSpec((seq, POOL_WIDTH), lambda b: (b, 0)),
        out_shape=jax.ShapeDtypeStruct((T, POOL_WIDTH), BF16),
        compiler_params=_cparams("parallel"),
        name="pool",
    )(proj, w_groups, scale.reshape(1, -1))


DA_AUG = 2 * DA_VDIM


def _diffattn_kernel(slopes_ref, q_ref, k_ref, v_ref, lam_ref, subln_ref, o_ref, ka_ref, va_ref, s_ref, *,
                     seq, blk, lambda_init):
    slope = slopes_ref[pl.program_id(1)]
    lp = lam_ref[...]
    lam = (jnp.exp(jnp.sum(lp[0:1] * lp[1:2], axis=-1, keepdims=True))
           - jnp.exp(jnp.sum(lp[2:3] * lp[3:4], axis=-1, keepdims=True)) + lambda_init)

    lane = lax.broadcasted_iota(jnp.int32, (seq, DA_VDIM), 1)
    kpos = lax.broadcasted_iota(jnp.int32, (seq, DA_VDIM), 0).astype(F32) * (slope * LOG2E)
    p1, p2, p3 = _split3_f32(kpos)
    ka_ref[:, 0:DA_VDIM] = k_ref[...].astype(BF16)
    ka_ref[:, DA_VDIM:DA_AUG] = jnp.where(lane == 0, p1, jnp.where(lane == 1, p2, jnp.where(lane == 2, p3, 0.0))
                                          ).astype(BF16)
    va_ref[:, 0:DA_VDIM] = v_ref[...].astype(BF16)
    va_ref[:, DA_VDIM:DA_AUG] = jnp.where(lane == 0, 1.0, 0.0).astype(BF16)

    qlane = lax.broadcasted_iota(jnp.int32, (blk, DA_VDIM), 1)
    ones3 = jnp.where(qlane < 3, 1.0, 0.0).astype(BF16)
    key_gt_query = (lax.broadcasted_iota(jnp.int32, (2 * blk, blk), 1)
                    > lax.broadcasted_iota(jnp.int32, (2 * blk, blk), 0) % blk)

    for i in range(seq // blk):
        rows = pl.ds(i * blk, blk)
        q = q_ref[rows, :] * (DA_HEAD_DIM ** -0.5 * LOG2E)
        q2 = jnp.concatenate(
            [jnp.concatenate([jnp.where(qlane < DA_HEAD_DIM, q, 0.0).astype(BF16), ones3], axis=1),
             jnp.concatenate([jnp.where(qlane >= DA_HEAD_DIM, q, 0.0).astype(BF16), ones3], axis=1)],
            axis=0)
        m_tile = jnp.full((2 * blk, DA_VDIM), -jnp.inf, F32)
        for j in range(i + 1):
            keys = pl.ds(j * blk, blk)
            s = _dot_nt(q2, ka_ref[keys, :])
            if j == i:
                s = jnp.where(key_gt_query, -jnp.inf, s)
            s_ref[:, keys] = s
            for c in range(blk // DA_VDIM):
                m_tile = jnp.maximum(m_tile, s[:, c * DA_VDIM:(c + 1) * DA_VDIM])
        m_row = jnp.max(m_tile, axis=-1, keepdims=True)
        acc = jnp.zeros((2 * blk, DA_AUG), F32)
        for j in range(i + 1):
            keys = pl.ds(j * blk, blk)
            p = jnp.exp2(s_ref[:, keys] - m_row).astype(BF16)
            acc += jnp.dot(p, va_ref[keys, :], preferred_element_type=F32)
        out0 = acc[0:blk, 0:DA_VDIM] * (1.0 / acc[0:blk, DA_VDIM:DA_VDIM + 1])
        out1 = acc[blk:2 * blk, 0:DA_VDIM] * (1.0 / acc[blk:2 * blk, DA_VDIM:DA_VDIM + 1])
        o = out0 - lam * out1
        o = o * lax.rsqrt(jnp.mean(o * o, axis=-1, keepdims=True) + NORM_EPS) * subln_ref[...]
        o_ref[rows, :] = (o * (1.0 - lambda_init)).astype(o_ref.dtype)


def diff_attention(proj, lam_params, subln, lambda_init, batch, seq, *, blk=256):
    T = proj.shape[0]
    slopes = (2.0 ** (-8.0 * jnp.arange(1, DA_HEADS + 1, dtype=F32) / DA_HEADS)).astype(F32)

    def col_spec(off):
        return pl.BlockSpec((seq, DA_VDIM), lambda b, h, s: (b, off // DA_VDIM + h))

    grid_spec = pltpu.PrefetchScalarGridSpec(
        num_scalar_prefetch=1,
        grid=(batch, DA_HEADS),
        in_specs=[col_spec(OFF_DQ), col_spec(OFF_DK), col_spec(OFF_DV),
                  pl.BlockSpec(lam_params.shape, lambda b, h, s: (0, 0)),
                  pl.BlockSpec((1, DA_VDIM), lambda b, h, s: (0, 0))],
        out_specs=pl.BlockSpec((seq, DA_VDIM), lambda b, h, s: (b, h)),
        scratch_shapes=[pltpu.VMEM((seq, DA_AUG), BF16), pltpu.VMEM((seq, DA_AUG), BF16),
                        pltpu.VMEM((2 * blk, seq), F32)],
    )
    return pl.pallas_call(
        functools.partial(_diffattn_kernel, seq=seq, blk=blk, lambda_init=lambda_init),
        grid_spec=grid_spec,
        out_shape=jax.ShapeDtypeStruct((T, DA_WIDTH), BF16),
        compiler_params=_cparams("parallel", "parallel"),
        name="diffattn",
    )(slopes, proj, proj, proj, lam_params, subln.reshape(1, -1))


def kernel(x, norm_mix_pre, norm_mix_post, norm_ffn_pre, norm_ffn_post, w_in, hgrn_lb_logits, hgrn_out_norm,
           pool_w, pool_scale, diff_lambda, diff_subln, w_up_a, w_up_b, w_up_c, w_out, w_ffn_gate, w_ffn_up,
           w_ffn_down):
    B, S, D = x.shape
    depth = w_in.shape[0]
    T = B * S
    lb_all = jnp.cumsum(jax.nn.softmax(hgrn_lb_logits.astype(F32), axis=0), axis=0)
    lb_all = lb_all - lb_all[0:1]

    x2 = x.reshape(T, D)
    h = prenorm(x2, norm_mix_pre[0])
    for l in range(depth):
        lambda_init = 0.8 - 0.6 * math.exp(-0.3 * l)
        proj = matmul_ws(h, w_in, l, bm=1024, bn=512, out_dtype=F32, name="in_proj")
        y_a = hgrn2_mixer(proj, lb_all[l], hgrn_out_norm[l], B, S)
        y_b = pool_mixer(proj, pool_w[l], pool_scale[l], B, S)
        y_c = diff_attention(proj, diff_lambda[l], diff_subln[l], lambda_init, B, S)
        merged = gated_merge(y_a, y_b, y_c, w_up_a, w_up_b, w_up_c, l, proj, D, bm=1024, bn=512)
        z = matmul_ws(merged, w_out, l, bm=1024, bn=512, out_dtype=F32, name="out_proj")
        x2, h = postnorm(x2, z, norm_mix_post[l], norm_ffn_pre[l])
        u, wd16 = swiglu_up(h, w_ffn_gate, w_ffn_up, w_ffn_down, l, bm=1024, bn=256)
        ff = matmul_bf16(u, wd16, bm=512, bn=512, out_dtype=F32, name="ffn_down")
        gnext = norm_mix_pre[l + 1] if l + 1 < depth else None
        x2, h = postnorm(x2, ff, norm_ffn_post[l], gnext)
    return x2.reshape(B, S, D)
```

```python
import functools
import math

import jax
import jax.numpy as jnp
import numpy as np
from jax import lax
from jax.experimental import pallas as pl
from jax.experimental.pallas import tpu as pltpu

HG_HEADS = 8
HG_DK = 128
HG_DV = 128
HG_WIDTH = HG_HEADS * HG_DV
HG_CHUNK = 64
POOL_WINDOWS = (2, 4, 8, 16)
POOL_GROUPS = 4
POOL_GROUP_DIM = 256
POOL_WIDTH = POOL_GROUPS * POOL_GROUP_DIM
DA_HEADS = 8
DA_HEAD_DIM = 64
DA_VDIM = 2 * DA_HEAD_DIM
DA_WIDTH = DA_HEADS * DA_VDIM
N_BRANCH = 3
NORM_EPS = 1e-6
LOG2E = 1.4426950408889634

OFF_HQ = 0
OFF_HF = OFF_HQ + HG_HEADS * HG_DK
OFF_HV = OFF_HF + HG_HEADS * HG_DK
OFF_HG = OFF_HV + HG_WIDTH
OFF_PU = OFF_HG + HG_WIDTH
OFF_DQ = OFF_PU + POOL_WIDTH
OFF_DK = OFF_DQ + DA_WIDTH
OFF_DV = OFF_DK + DA_WIDTH
OFF_GATE = OFF_DV + DA_WIDTH

SUBLANES = 8
V7X_VMEM_LIMIT_BYTES = 56 * 1024 * 1024

BF16 = jnp.bfloat16
F32 = jnp.float32


def _cparams(*sem):
    return pltpu.CompilerParams(dimension_semantics=sem, vmem_limit_bytes=V7X_VMEM_LIMIT_BYTES)


def _rms(x, gain):
    return x * lax.rsqrt(jnp.mean(x * x, axis=-1, keepdims=True) + NORM_EPS) * gain


def _prenorm_kernel(x_ref, g_ref, h_ref):
    h_ref[...] = _rms(x_ref[...], g_ref[...]).astype(h_ref.dtype)


def prenorm(x2, gain, *, rows=256):
    T, D = x2.shape
    return pl.pallas_call(
        _prenorm_kernel,
        grid=(T // rows,),
        in_specs=[pl.BlockSpec((rows, D), lambda i: (i, 0)),
                  pl.BlockSpec((1, D), lambda i: (0, 0))],
        out_specs=pl.BlockSpec((rows, D), lambda i: (i, 0)),
        out_shape=jax.ShapeDtypeStruct((T, D), BF16),
        compiler_params=_cparams("parallel"),
        name="prenorm",
    )(x2, gain.reshape(1, D))


def _postnorm_kernel(x_ref, z_ref, gpost_ref, gnext_ref, xo_ref, h_ref):
    xn = x_ref[...] + _rms(z_ref[...], gpost_ref[...])
    xo_ref[...] = xn
    h_ref[...] = _rms(xn, gnext_ref[...]).astype(h_ref.dtype)


def _postnorm_last_kernel(x_ref, z_ref, gpost_ref, xo_ref):
    xo_ref[...] = x_ref[...] + _rms(z_ref[...], gpost_ref[...])


def postnorm(x2, z, gpost, gnext, *, rows=256):
    T, D = x2.shape
    row_spec = pl.BlockSpec((rows, D), lambda i: (i, 0))
    vec_spec = pl.BlockSpec((1, D), lambda i: (0, 0))
    if gnext is None:
        return pl.pallas_call(
            _postnorm_last_kernel,
            grid=(T // rows,),
            in_specs=[row_spec, row_spec, vec_spec],
            out_specs=row_spec,
            out_shape=jax.ShapeDtypeStruct((T, D), F32),
            compiler_params=_cparams("parallel"),
            name="postnorm_last",
        )(x2, z, gpost.reshape(1, D)), None
    return pl.pallas_call(
        _postnorm_kernel,
        grid=(T // rows,),
        in_specs=[row_spec, row_spec, vec_spec, vec_spec],
        out_specs=[row_spec, row_spec],
        out_shape=[jax.ShapeDtypeStruct((T, D), F32), jax.ShapeDtypeStruct((T, D), BF16)],
        compiler_params=_cparams("parallel"),
        name="postnorm",
    )(x2, z, gpost.reshape(1, D), gnext.reshape(1, D))


def _first_row_tile():
    return pl.program_id(1) == 0


def _stream_weight_tiles(w_hbm, w16_ref, stage_ref, sem, *, layer, n_col_tiles):
    j, i = pl.program_id(0), pl.program_id(1)
    _, k_total, bn = w16_ref.shape
    kc = stage_ref.shape[1]
    n_chunks = k_total // kc
    g = j * n_chunks + i
    n_stream = (n_col_tiles - 1) * n_chunks

    def copy(tile, chunk, slot):
        return pltpu.make_async_copy(w_hbm.at[layer, pl.ds(chunk * kc, kc), pl.ds(tile * bn, bn)],
                                     stage_ref.at[slot], sem.at[slot])

    @pl.when(g == 0)
    def _first_tile():
        copy(0, 0, 0).start()
        for c in range(n_chunks):
            if c + 1 < n_chunks:
                copy(0, c + 1, (c + 1) % 2).start()
            copy(0, c, c % 2).wait()
            w16_ref[0, c * kc:(c + 1) * kc, :] = stage_ref[c % 2].astype(BF16)
        if n_stream > 0:
            copy(1, 0, 0).start()

    @pl.when(g < n_stream)
    def _next_tile_chunk():
        nxt = g + 1

        @pl.when(nxt < n_stream)
        def _():
            copy(1 + nxt // n_chunks, nxt % n_chunks, nxt % 2).start()

        copy(j + 1, i, g % 2).wait()
        w16_ref[(j + 1) % 2, pl.ds(pl.multiple_of(i * kc, kc), kc), :] = stage_ref[g % 2].astype(BF16)


def _mm_ws_kernel(a_ref, w_hbm, o_ref, w16_ref, stage_ref, sem, *, layer, n_col_tiles):
    _stream_weight_tiles(w_hbm, w16_ref, stage_ref, sem, layer=layer, n_col_tiles=n_col_tiles)
    w16 = w16_ref[pl.program_id(0) % 2]
    o_ref[...] = jnp.dot(a_ref[...], w16, preferred_element_type=F32).astype(o_ref.dtype)


def _weight_stream_scratch(k_total, bn, n_row_tiles):
    assert k_total % n_row_tiles == 0 and n_row_tiles % 2 == 0
    return [pltpu.VMEM((2, k_total, bn), BF16), pltpu.VMEM((2, k_total // n_row_tiles, bn), F32),
            pltpu.SemaphoreType.DMA((2,))]


def matmul_ws(a, w_all, layer, *, bm, bn, out_dtype, name):
    M, K = a.shape
    N = w_all.shape[2]
    return pl.pallas_call(
        functools.partial(_mm_ws_kernel, layer=layer, n_col_tiles=N // bn),
        grid=(N // bn, M // bm),
        in_specs=[pl.BlockSpec((bm, K), lambda j, i: (i, 0)),
                  pl.BlockSpec(memory_space=pl.ANY)],
        out_specs=pl.BlockSpec((bm, bn), lambda j, i: (i, j)),
        out_shape=jax.ShapeDtypeStruct((M, N), out_dtype),
        scratch_shapes=_weight_stream_scratch(K, bn, M // bm),
        compiler_params=_cparams("arbitrary", "arbitrary"),
        name=name,
    )(a, w_all)


def _mm_kernel(a_ref, w_ref, o_ref):
    o_ref[...] = jnp.dot(a_ref[...], w_ref[...], preferred_element_type=F32).astype(o_ref.dtype)


def matmul_bf16(a, w, *, bm, bn, out_dtype, name):
    M, K = a.shape
    N = w.shape[1]
    return pl.pallas_call(
        _mm_kernel,
        grid=(M // bm, N // bn),
        in_specs=[pl.BlockSpec((bm, K), lambda i, j: (i, 0)),
                  pl.BlockSpec((K, bn), lambda i, j: (0, j))],
        out_specs=pl.BlockSpec((bm, bn), lambda i, j: (i, j)),
        out_shape=jax.ShapeDtypeStruct((M, N), out_dtype),
        compiler_params=_cparams("parallel", "arbitrary"),
        name=name,
    )(a, w)


def _cast_rows_stream(src_hbm, out_ref, stage_ref, sem, *, layer):
    j, i = pl.program_id(0), pl.program_id(1)
    n_row_tiles = pl.num_programs(1)
    g = j * n_row_tiles + i
    n_steps = pl.num_programs(0) * n_row_tiles
    r = stage_ref.shape[1]

    def copy(step, slot):
        return pltpu.make_async_copy(src_hbm.at[layer, pl.ds(step * r, r), :], stage_ref.at[slot], sem.at[slot])

    @pl.when(g == 0)
    def _():
        copy(0, 0).start()

    @pl.when(g + 1 < n_steps)
    def _():
        copy(g + 1, (g + 1) % 2).start()

    copy(g, g % 2).wait()
    out_ref[pl.ds(pl.multiple_of(i * r, r), r), :] = stage_ref[g % 2].astype(BF16)


def _swiglu_kernel(h_ref, wg_hbm, wu_hbm, wd_hbm, o_ref, wd16_ref,
                   wg16_ref, wg_stage, wg_sem, wu16_ref, wu_stage, wu_sem, wd_stage, wd_sem, *,
                   layer, n_col_tiles):
    _stream_weight_tiles(wg_hbm, wg16_ref, wg_stage, wg_sem, layer=layer, n_col_tiles=n_col_tiles)
    _stream_weight_tiles(wu_hbm, wu16_ref, wu_stage, wu_sem, layer=layer, n_col_tiles=n_col_tiles)
    _cast_rows_stream(wd_hbm, wd16_ref, wd_stage, wd_sem, layer=layer)
    slot = pl.program_id(0) % 2
    h = h_ref[...]
    a = jnp.dot(h, wg16_ref[slot], preferred_element_type=F32)
    b = jnp.dot(h, wu16_ref[slot], preferred_element_type=F32)
    o_ref[...] = (a * jax.nn.sigmoid(a) * b).astype(o_ref.dtype)


def swiglu_up(h, wg_all, wu_all, wd_all, layer, *, bm, bn):
    M, K = h.shape
    F = wg_all.shape[2]
    D = wd_all.shape[2]
    n_row_tiles = M // bm
    assert F % bn == 0 and bn % n_row_tiles == 0
    any_spec = pl.BlockSpec(memory_space=pl.ANY)
    return pl.pallas_call(
        functools.partial(_swiglu_kernel, layer=layer, n_col_tiles=F // bn),
        grid=(F // bn, n_row_tiles),
        in_specs=[pl.BlockSpec((bm, K), lambda j, i: (i, 0)), any_spec, any_spec, any_spec],
        out_specs=[pl.BlockSpec((bm, bn), lambda j, i: (i, j)),
                   pl.BlockSpec((bn, D), lambda j, i: (j, 0))],
        out_shape=[jax.ShapeDtypeStruct((M, F), BF16), jax.ShapeDtypeStruct((F, D), BF16)],
        scratch_shapes=(_weight_stream_scratch(K, bn, n_row_tiles) + _weight_stream_scratch(K, bn, n_row_tiles)
                        + [pltpu.VMEM((2, bn // n_row_tiles, D), F32), pltpu.SemaphoreType.DMA((2,))]),
        compiler_params=_cparams("arbitrary", "arbitrary"),
        name="swiglu_up",
    )(h, wg_all, wu_all, wd_all)


def _merge_kernel(ya_ref, yb_ref, yc_ref, wa_ref, wb_ref, wc_ref, ga_ref, gb_ref, gc_ref, o_ref,
                  wa16_ref, wb16_ref, wc16_ref):
    @pl.when(_first_row_tile())
    def _():
        wa16_ref[...] = wa_ref[...].astype(BF16)
        wb16_ref[...] = wb_ref[...].astype(BF16)
        wc16_ref[...] = wc_ref[...].astype(BF16)

    acc = jax.nn.sigmoid(ga_ref[...]) * jnp.dot(ya_ref[...], wa16_ref[...], preferred_element_type=F32)
    acc += jax.nn.sigmoid(gb_ref[...]) * jnp.dot(yb_ref[...], wb16_ref[...], preferred_element_type=F32)
    acc += jax.nn.sigmoid(gc_ref[...]) * jnp.dot(yc_ref[...], wc16_ref[...], preferred_element_type=F32)
    o_ref[...] = acc.astype(o_ref.dtype)


def gated_merge(ya, yb, yc, wa_all, wb_all, wc_all, layer, proj, d_model, *, bm, bn):
    M, Kb = ya.shape
    gate_blk = OFF_GATE // bn
    per_branch = d_model // bn
    y_spec = pl.BlockSpec((bm, Kb), lambda j, i: (i, 0))
    w_spec = pl.BlockSpec((None, Kb, bn), lambda j, i: (layer, 0, j))

    def g_spec(k):
        return pl.BlockSpec((bm, bn), lambda j, i: (i, gate_blk + k * per_branch + j))

    return pl.pallas_call(
        _merge_kernel,
        grid=(d_model // bn, M // bm),
        in_specs=[y_spec, y_spec, y_spec, w_spec, w_spec, w_spec, g_spec(0), g_spec(1), g_spec(2)],
        out_specs=pl.BlockSpec((bm, bn), lambda j, i: (i, j)),
        out_shape=jax.ShapeDtypeStruct((M, d_model), BF16),
        scratch_shapes=[pltpu.VMEM((Kb, bn), BF16)] * 3,
        compiler_params=_cparams("arbitrary", "arbitrary"),
        name="gated_merge",
    )(ya, yb, yc, wa_all, wb_all, wc_all, proj, proj, proj)


def _hgrn_tables(chunk):
    levels = int(math.log2(chunk))
    assert 1 << levels == chunk and levels >= 3
    t = np.arange(chunk)[:, None]
    u = np.arange(chunk)[None, :]
    masks = [(t == u)]
    for l in range(levels):
        m = 1 << l
        lower_t = (t % (2 * m)) >= m
        upper_u = (u % (2 * m)) < m
        masks.append(lower_t & upper_u & (t // (2 * m) == u // (2 * m)))
    return (u <= t).astype(np.float32), np.stack([m.astype(np.float32) for m in masks], axis=0)


def _split3_f32(x):
    p1 = x.astype(BF16).astype(F32)
    r1 = x - p1
    p2 = r1.astype(BF16).astype(F32)
    return p1, p2, r1 - p2


def _dot_nt(a, b):
    return lax.dot_general(a, b, (((1,), (1,)), ((), ())), preferred_element_type=F32)


def _hgrn_level_decays(f, b, chunk):
    dk = f.shape[1]
    row = lax.broadcasted_iota(jnp.int32, f.shape, 0)
    f_prev = pltpu.roll(f, 1, axis=0)
    f_next = pltpu.roll(f, chunk - 1, axis=0)
    r4 = row % 4
    decays = [jnp.where(row % 2 == 1, f, 1.0),
              jnp.where(r4 == 0, f_next, jnp.where(r4 == 1, 1.0, jnp.where(r4 == 2, f, f * f_prev)))]
    m = 4
    while m < chunk:
        ref = jnp.concatenate(
            [jnp.broadcast_to(b[p * 2 * m + m - 1:p * 2 * m + m, :], (2 * m, dk)) for p in range(chunk // (2 * m))],
            axis=0)
        lower = (row % (2 * m)) >= m
        decays.append(jnp.exp(jnp.where(lower, b - ref, ref - b)))
        m *= 2
    return decays


def _hgrn_kernel(q_ref, f_ref, v_ref, g_ref, lb_ref, gain_ref, tri_ref, masks_ref, o_ref, *,
                 chunk, n_chunks, unroll):
    dk = q_ref.shape[1]
    lb = lb_ref[...]
    gain = gain_ref[...]
    tri = tri_ref[...].astype(BF16)

    def body(c, st):
        rows = pl.ds(pl.multiple_of(c * chunk, chunk), chunk)
        q = q_ref[rows, :] * (HG_DK ** -0.5)
        f = lb + (1.0 - lb) * jax.nn.sigmoid(f_ref[rows, :])
        kk = 1.0 - f
        v16 = v_ref[rows, :].astype(BF16)
        parts = jnp.concatenate([p.astype(BF16) for p in _split3_f32(jnp.log(f))], axis=1)
        b3 = jnp.dot(tri, parts, preferred_element_type=F32)
        b = b3[:, :dk] + b3[:, dk:2 * dk] + b3[:, 2 * dk:]
        b_last = b[chunk - 1:chunk, :]
        q_inter = (q * jnp.exp(b)).astype(BF16)
        k_inter = (kk * jnp.exp(b_last - b)).astype(BF16)
        scores = masks_ref[0] * _dot_nt(q.astype(BF16), kk.astype(BF16))
        for l, xl in enumerate(_hgrn_level_decays(f, b, chunk)):
            scores += masks_ref[1 + l] * _dot_nt((q * xl).astype(BF16), (kk * xl).astype(BF16))
        o = jnp.dot(scores.astype(BF16), v16, preferred_element_type=F32)
        o += _dot_nt(q_inter, st.astype(BF16))
        upd = lax.dot_general(v16, k_inter, (((0,), (0,)), ((), ())), preferred_element_type=F32)
        o = o * lax.rsqrt(jnp.mean(o * o, axis=-1, keepdims=True) + NORM_EPS) * gain
        gate = g_ref[rows, :]
        o_ref[rows, :] = (o * (gate * jax.nn.sigmoid(gate))).astype(o_ref.dtype)
        return st * jnp.exp(b_last) + upd

    lax.fori_loop(0, n_chunks, body, jnp.zeros((HG_DV, dk), F32), unroll=unroll)


def hgrn2_mixer(proj, lb, out_gain, batch, seq, *, chunk=256, unroll=2):
    T = proj.shape[0]
    tri, masks = _hgrn_tables(chunk)

    def col_spec(off):
        return pl.BlockSpec((seq, HG_DK), lambda b, h: (b, off // HG_DK + h))

    vec_spec = pl.BlockSpec((1, HG_DK), lambda b, h: (0, h))
    return pl.pallas_call(
        functools.partial(_hgrn_kernel, chunk=chunk, n_chunks=seq // chunk, unroll=unroll),
        grid=(batch, HG_HEADS),
        in_specs=[col_spec(OFF_HQ), col_spec(OFF_HF), col_spec(OFF_HV), col_spec(OFF_HG),
                  vec_spec, vec_spec,
                  pl.BlockSpec(tri.shape, lambda b, h: (0, 0)),
                  pl.BlockSpec(masks.shape, lambda b, h: (0, 0, 0))],
        out_specs=pl.BlockSpec((seq, HG_DV), lambda b, h: (b, h)),
        out_shape=jax.ShapeDtypeStruct((T, HG_WIDTH), BF16),
        compiler_params=_cparams("parallel", "parallel"),
        name="hgrn2",
    )(proj, proj, proj, proj, lb.reshape(1, -1), out_gain.reshape(1, -1),
      jnp.asarray(tri), jnp.asarray(masks))


def _pool_kernel(u_ref, w_ref, scale_ref, o_ref):
    seq = u_ref.shape[0]
    row = lax.broadcasted_iota(jnp.int32, (seq, POOL_GROUP_DIM), 0)
    for j, window in enumerate(POOL_WINDOWS):
        cols = slice(j * POOL_GROUP_DIM, (j + 1) * POOL_GROUP_DIM)
        u = u_ref[:, cols]
        acc = u
        span = 1
        while span < window:
            shifted = jnp.where(row >= span, pltpu.roll(acc, span, axis=0), 0.0)
            acc = acc + shifted
            span *= 2
        count = jnp.minimum(row + 1, window).astype(F32)
        pooled = acc / count - u
        mixed = jnp.dot(pooled.astype(BF16), w_ref[j].astype(BF16), preferred_element_type=F32)
        o_ref[:, cols] = (mixed * scale_ref[:, cols]).astype(o_ref.dtype)


def pool_mixer(proj, w_groups, scale, batch, seq):
    T = proj.shape[0]
    assert all(w & (w - 1) == 0 for w in POOL_WINDOWS)
    return pl.pallas_call(
        _pool_kernel,
        grid=(batch,),
        in_specs=[pl.BlockSpec((seq, POOL_WIDTH), lambda b: (b, OFF_PU // POOL_WIDTH)),
                  pl.BlockSpec(w_groups.shape, lambda b: (0, 0, 0)),
                  pl.BlockSpec((1, POOL_WIDTH), lambda b: (0, 0))],
        out_specs=pl.BlockSpec((seq, POOL_WIDTH), lambda b: (b, 0)),
        out_shape=jax.ShapeDtypeStruct((T, POOL_WIDTH), BF16),
        compiler_params=_cparams("parallel"),
        name="pool",
    )(proj, w_groups, scale.reshape(1, -1))


DA_AUG = 2 * DA_VDIM


def _diffattn_kernel(slopes_ref, q_ref, k_ref, v_ref, lam_ref, subln_ref, o_ref, ka_ref, va_ref, s_ref, *,
                     seq, blk, lambda_init):
    slope = slopes_ref[pl.program_id(1)]
    lp = lam_ref[...]
    lam = (jnp.exp(jnp.sum(lp[0:1] * lp[1:2], axis=-1, keepdims=True))
           - jnp.exp(jnp.sum(lp[2:3] * lp[3:4], axis=-1, keepdims=True)) + lambda_init)

    lane = lax.broadcasted_iota(jnp.int32, (seq, DA_VDIM), 1)
    kpos = lax.broadcasted_iota(jnp.int32, (seq, DA_VDIM), 0).astype(F32) * (slope * LOG2E)
    p1, p2, p3 = _split3_f32(kpos)
    ka_ref[:, 0:DA_VDIM] = k_ref[...].astype(BF16)
    ka_ref[:, DA_VDIM:DA_AUG] = jnp.where(lane == 0, p1, jnp.where(lane == 1, p2, jnp.where(lane == 2, p3, 0.0))
                                          ).astype(BF16)
    va_ref[:, 0:DA_VDIM] = v_ref[...].astype(BF16)
    va_ref[:, DA_VDIM:DA_AUG] = jnp.where(lane == 0, 1.0, 0.0).astype(BF16)

    qlane = lax.broadcasted_iota(jnp.int32, (blk, DA_VDIM), 1)
    ones3 = jnp.where(qlane < 3, 1.0, 0.0).astype(BF16)
    key_gt_query = (lax.broadcasted_iota(jnp.int32, (2 * blk, blk), 1)
                    > lax.broadcasted_iota(jnp.int32, (2 * blk, blk), 0) % blk)

    for i in range(seq // blk):
        rows = pl.ds(i * blk, blk)
        q = q_ref[rows, :] * (DA_HEAD_DIM ** -0.5 * LOG2E)
        q2 = jnp.concatenate(
            [jnp.concatenate([jnp.where(qlane < DA_HEAD_DIM, q, 0.0).astype(BF16), ones3], axis=1),
             jnp.concatenate([jnp.where(qlane >= DA_HEAD_DIM, q, 0.0).astype(BF16), ones3], axis=1)],
            axis=0)
        m_tile = jnp.full((2 * blk, DA_VDIM), -jnp.inf, F32)
        for j in range(i + 1):
            keys = pl.ds(j * blk, blk)
            s = _dot_nt(q2, ka_ref[keys, :])
            if j == i:
                s = jnp.where(key_gt_query, -jnp.inf, s)
            s_ref[:, keys] = s
            for c in range(blk // DA_VDIM):
                m_tile = jnp.maximum(m_tile, s[:, c * DA_VDIM:(c + 1) * DA_VDIM])
        m_row = jnp.max(m_tile, axis=-1, keepdims=True)
        acc = jnp.zeros((2 * blk, DA_AUG), F32)
        for j in range(i + 1):
            keys = pl.ds(j * blk, blk)
            p = jnp.exp2(s_ref[:, keys] - m_row).astype(BF16)
            acc += jnp.dot(p, va_ref[keys, :], preferred_element_type=F32)
        out0 = acc[0:blk, 0:DA_VDIM] * (1.0 / acc[0:blk, DA_VDIM:DA_VDIM + 1])
        out1 = acc[blk:2 * blk, 0:DA_VDIM] * (1.0 / acc[blk:2 * blk, DA_VDIM:DA_VDIM + 1])
        o = out0 - lam * out1
        o = o * lax.rsqrt(jnp.mean(o * o, axis=-1, keepdims=True) + NORM_EPS) * subln_ref[...]
        o_ref[rows, :] = (o * (1.0 - lambda_init)).astype(o_ref.dtype)


def diff_attention(proj, lam_params, subln, lambda_init, batch, seq, *, blk=256):
    T = proj.shape[0]
    slopes = (2.0 ** (-8.0 * jnp.arange(1, DA_HEADS + 1, dtype=F32) / DA_HEADS)).astype(F32)

    def col_spec(off):
        return pl.BlockSpec((seq, DA_VDIM), lambda b, h, s: (b, off // DA_VDIM + h))

    grid_spec = pltpu.PrefetchScalarGridSpec(
        num_scalar_prefetch=1,
        grid=(batch, DA_HEADS),
        in_specs=[col_spec(OFF_DQ), col_spec(OFF_DK), col_spec(OFF_DV),
                  pl.BlockSpec(lam_params.shape, lambda b, h, s: (0, 0)),
                  pl.BlockSpec((1, DA_VDIM), lambda b, h, s: (0, 0))],
        out_specs=pl.BlockSpec((seq, DA_VDIM), lambda b, h, s: (b, h)),
        scratch_shapes=[pltpu.VMEM((seq, DA_AUG), BF16), pltpu.VMEM((seq, DA_AUG), BF16),
                        pltpu.VMEM((2 * blk, seq), F32)],
    )
    return pl.pallas_call(
        functools.partial(_diffattn_kernel, seq=seq, blk=blk, lambda_init=lambda_init),
        grid_spec=grid_spec,
        out_shape=jax.ShapeDtypeStruct((T, DA_WIDTH), BF16),
        compiler_params=_cparams("parallel", "parallel"),
        name="diffattn",
    )(slopes, proj, proj, proj, lam_params, subln.reshape(1, -1))


def kernel(x, norm_mix_pre, norm_mix_post, norm_ffn_pre, norm_ffn_post, w_in, hgrn_lb_logits, hgrn_out_norm,
           pool_w, pool_scale, diff_lambda, diff_subln, w_up_a, w_up_b, w_up_c, w_out, w_ffn_gate, w_ffn_up,
           w_ffn_down):
    B, S, D = x.shape
    depth = w_in.shape[0]
    T = B * S
    lb_all = jnp.cumsum(jax.nn.softmax(hgrn_lb_logits.astype(F32), axis=0), axis=0)
    lb_all = lb_all - lb_all[0:1]

    x2 = x.reshape(T, D)
    h = prenorm(x2, norm_mix_pre[0])
    for l in range(depth):
        lambda_init = 0.8 - 0.6 * math.exp(-0.3 * l)
        proj = matmul_ws(h, w_in, l, bm=1024, bn=1024, out_dtype=F32, name="in_proj")
        y_a = hgrn2_mixer(proj, lb_all[l], hgrn_out_norm[l], B, S)
        y_b = pool_mixer(proj, pool_w[l], pool_scale[l], B, S)
        y_c = diff_attention(proj, diff_lambda[l], diff_subln[l], lambda_init, B, S)
        merged = gated_merge(y_a, y_b, y_c, w_up_a, w_up_b, w_up_c, l, proj, D, bm=1024, bn=512)
        z = matmul_ws(merged, w_out, l, bm=1024, bn=1024, out_dtype=F32, name="out_proj")
        x2, h = postnorm(x2, z, norm_mix_post[l], norm_ffn_pre[l])
        u, wd16 = swiglu_up(h, w_ffn_gate, w_ffn_up, w_ffn_down, l, bm=1024, bn=256)
        ff = matmul_bf16(u, wd16, bm=512, bn=512, out_dtype=F32, name="ffn_down")
        gnext = norm_mix_pre[l + 1] if l + 1 < depth else None
        x2, h = postnorm(x2, ff, norm_ffn_post[l], gnext)
    return x2.reshape(B, S, D)
```

```python
import functools
import math

import jax
import jax.numpy as jnp
import numpy as np
from jax import lax
from jax.experimental import pallas as pl
from jax.experimental.pallas import tpu as pltpu

HG_HEADS = 8
HG_DK = 128
HG_DV = 128
HG_WIDTH = HG_HEADS * HG_DV
HG_CHUNK = 64
POOL_WINDOWS = (2, 4, 8, 16)
POOL_GROUPS = 4
POOL_GROUP_DIM = 256
POOL_WIDTH = POOL_GROUPS * POOL_GROUP_DIM
DA_HEADS = 8
DA_HEAD_DIM = 64
DA_VDIM = 2 * DA_HEAD_DIM
DA_WIDTH = DA_HEADS * DA_VDIM
N_BRANCH = 3
NORM_EPS = 1e-6
LOG2E = 1.4426950408889634

OFF_HQ = 0
OFF_HF = OFF_HQ + HG_HEADS * HG_DK
OFF_HV = OFF_HF + HG_HEADS * HG_DK
OFF_HG = OFF_HV + HG_WIDTH
OFF_PU = OFF_HG + HG_WIDTH
OFF_DQ = OFF_PU + POOL_WIDTH
OFF_DK = OFF_DQ + DA_WIDTH
OFF_DV = OFF_DK + DA_WIDTH
OFF_GATE = OFF_DV + DA_WIDTH

SUBLANES = 8
V7X_VMEM_LIMIT_BYTES = 56 * 1024 * 1024

BF16 = jnp.bfloat16
F32 = jnp.float32


def _cparams(*sem):
    return pltpu.CompilerParams(dimension_semantics=sem, vmem_limit_bytes=V7X_VMEM_LIMIT_BYTES)


def _rms(x, gain):
    return x * lax.rsqrt(jnp.mean(x * x, axis=-1, keepdims=True) + NORM_EPS) * gain


def _prenorm_kernel(x_ref, g_ref, h_ref):
    h_ref[...] = _rms(x_ref[...], g_ref[...]).astype(h_ref.dtype)


def prenorm(x2, gain, *, rows=256):
    T, D = x2.shape
    return pl.pallas_call(
        _prenorm_kernel,
        grid=(T // rows,),
        in_specs=[pl.BlockSpec((rows, D), lambda i: (i, 0)),
                  pl.BlockSpec((1, D), lambda i: (0, 0))],
        out_specs=pl.BlockSpec((rows, D), lambda i: (i, 0)),
        out_shape=jax.ShapeDtypeStruct((T, D), BF16),
        compiler_params=_cparams("parallel"),
        name="prenorm",
    )(x2, gain.reshape(1, D))


def _postnorm_kernel(x_ref, z_ref, gpost_ref, gnext_ref, xo_ref, h_ref):
    xn = x_ref[...] + _rms(z_ref[...], gpost_ref[...])
    xo_ref[...] = xn
    h_ref[...] = _rms(xn, gnext_ref[...]).astype(h_ref.dtype)


def _postnorm_last_kernel(x_ref, z_ref, gpost_ref, xo_ref):
    xo_ref[...] = x_ref[...] + _rms(z_ref[...], gpost_ref[...])


def postnorm(x2, z, gpost, gnext, *, rows=256):
    T, D = x2.shape
    row_spec = pl.BlockSpec((rows, D), lambda i: (i, 0))
    vec_spec = pl.BlockSpec((1, D), lambda i: (0, 0))
    if gnext is None:
        return pl.pallas_call(
            _postnorm_last_kernel,
            grid=(T // rows,),
            in_specs=[row_spec, row_spec, vec_spec],
            out_specs=row_spec,
            out_shape=jax.ShapeDtypeStruct((T, D), F32),
            compiler_params=_cparams("parallel"),
            name="postnorm_last",
        )(x2, z, gpost.reshape(1, D)), None
    return pl.pallas_call(
        _postnorm_kernel,
        grid=(T // rows,),
        in_specs=[row_spec, row_spec, vec_spec, vec_spec],
        out_specs=[row_spec, row_spec],
        out_shape=[jax.ShapeDtypeStruct((T, D), F32), jax.ShapeDtypeStruct((T, D), BF16)],
        compiler_params=_cparams("parallel"),
        name="postnorm",
    )(x2, z, gpost.reshape(1, D), gnext.reshape(1, D))


def _stream_weight_tiles(w_hbm, w16_ref, stage_ref, sem, *, layer, n_col_tiles, col0=0):
    j, i = pl.program_id(0), pl.program_id(1)
    _, k_total, bn = w16_ref.shape
    kc = stage_ref.shape[1]
    n_chunks = k_total // kc
    g = j * n_chunks + i
    n_stream = (n_col_tiles - 1) * n_chunks

    def copy(tile, chunk, slot):
        return pltpu.make_async_copy(w_hbm.at[layer, pl.ds(chunk * kc, kc), pl.ds(col0 + tile * bn, bn)],
                                     stage_ref.at[slot], sem.at[slot])

    @pl.when(g == 0)
    def _first_tile():
        copy(0, 0, 0).start()
        for c in range(n_chunks):
            if c + 1 < n_chunks:
                copy(0, c + 1, (c + 1) % 2).start()
            copy(0, c, c % 2).wait()
            w16_ref[0, c * kc:(c + 1) * kc, :] = stage_ref[c % 2].astype(BF16)
        if n_stream > 0:
            copy(1, 0, 0).start()

    @pl.when(g < n_stream)
    def _next_tile_chunk():
        nxt = g + 1

        @pl.when(nxt < n_stream)
        def _():
            copy(1 + nxt // n_chunks, nxt % n_chunks, nxt % 2).start()

        copy(j + 1, i, g % 2).wait()
        w16_ref[(j + 1) % 2, pl.ds(pl.multiple_of(i * kc, kc), kc), :] = stage_ref[g % 2].astype(BF16)


def _mm_ws_kernel(a_ref, w_hbm, o_ref, w16_ref, stage_ref, sem, *, layer, n_col_tiles, col0, sigmoid_out):
    _stream_weight_tiles(w_hbm, w16_ref, stage_ref, sem, layer=layer, n_col_tiles=n_col_tiles, col0=col0)
    w16 = w16_ref[pl.program_id(0) % 2]
    acc = jnp.dot(a_ref[...], w16, preferred_element_type=F32)
    if sigmoid_out:
        acc = jax.nn.sigmoid(acc)
    o_ref[...] = acc.astype(o_ref.dtype)


def _weight_stream_scratch(k_total, bn, n_row_tiles):
    assert k_total % n_row_tiles == 0 and n_row_tiles % 2 == 0
    return [pltpu.VMEM((2, k_total, bn), BF16), pltpu.VMEM((2, k_total // n_row_tiles, bn), F32),
            pltpu.SemaphoreType.DMA((2,))]


def matmul_ws(a, w_all, layer, *, bm, bn, out_dtype, name, col0=0, n_cols=None, sigmoid_out=False):
    M, K = a.shape
    N = w_all.shape[2] - col0 if n_cols is None else n_cols
    assert N % bn == 0 and M % bm == 0
    return pl.pallas_call(
        functools.partial(_mm_ws_kernel, layer=layer, n_col_tiles=N // bn, col0=col0, sigmoid_out=sigmoid_out),
        grid=(N // bn, M // bm),
        in_specs=[pl.BlockSpec((bm, K), lambda j, i: (i, 0)),
                  pl.BlockSpec(memory_space=pl.ANY)],
        out_specs=pl.BlockSpec((bm, bn), lambda j, i: (i, j)),
        out_shape=jax.ShapeDtypeStruct((M, N), out_dtype),
        scratch_shapes=_weight_stream_scratch(K, bn, M // bm),
        compiler_params=_cparams("arbitrary", "arbitrary"),
        name=name,
    )(a, w_all)


def _mm_kernel(a_ref, w_ref, o_ref):
    o_ref[...] = jnp.dot(a_ref[...], w_ref[...], preferred_element_type=F32).astype(o_ref.dtype)


def matmul_bf16(a, w, *, bm, bn, out_dtype, name):
    M, K = a.shape
    N = w.shape[1]
    return pl.pallas_call(
        _mm_kernel,
        grid=(M // bm, N // bn),
        in_specs=[pl.BlockSpec((bm, K), lambda i, j: (i, 0)),
                  pl.BlockSpec((K, bn), lambda i, j: (0, j))],
        out_specs=pl.BlockSpec((bm, bn), lambda i, j: (i, j)),
        out_shape=jax.ShapeDtypeStruct((M, N), out_dtype),
        compiler_params=_cparams("parallel", "arbitrary"),
        name=name,
    )(a, w)


def _cast_rows_stream(src_hbm, out_ref, stage_ref, sem, *, layer):
    j, i = pl.program_id(0), pl.program_id(1)
    n_row_tiles = pl.num_programs(1)
    g = j * n_row_tiles + i
    n_steps = pl.num_programs(0) * n_row_tiles
    r = stage_ref.shape[1]

    def copy(step, slot):
        return pltpu.make_async_copy(src_hbm.at[layer, pl.ds(step * r, r), :], stage_ref.at[slot], sem.at[slot])

    @pl.when(g == 0)
    def _():
        copy(0, 0).start()

    @pl.when(g + 1 < n_steps)
    def _():
        copy(g + 1, (g + 1) % 2).start()

    copy(g, g % 2).wait()
    out_ref[pl.ds(pl.multiple_of(i * r, r), r), :] = stage_ref[g % 2].astype(BF16)


def _swiglu_kernel(h_ref, wg_hbm, wu_hbm, wd_hbm, o_ref, wd16_ref,
                   wg16_ref, wg_stage, wg_sem, wu16_ref, wu_stage, wu_sem, wd_stage, wd_sem, *,
                   layer, n_col_tiles):
    _stream_weight_tiles(wg_hbm, wg16_ref, wg_stage, wg_sem, layer=layer, n_col_tiles=n_col_tiles)
    _stream_weight_tiles(wu_hbm, wu16_ref, wu_stage, wu_sem, layer=layer, n_col_tiles=n_col_tiles)
    _cast_rows_stream(wd_hbm, wd16_ref, wd_stage, wd_sem, layer=layer)
    slot = pl.program_id(0) % 2
    h = h_ref[...]
    a = jnp.dot(h, wg16_ref[slot], preferred_element_type=F32)
    b = jnp.dot(h, wu16_ref[slot], preferred_element_type=F32)
    o_ref[...] = (a * jax.nn.sigmoid(a) * b).astype(o_ref.dtype)


def swiglu_up(h, wg_all, wu_all, wd_all, layer, *, bm, bn):
    M, K = h.shape
    F = wg_all.shape[2]
    D = wd_all.shape[2]
    n_row_tiles = M // bm
    assert F % bn == 0 and bn % n_row_tiles == 0
    any_spec = pl.BlockSpec(memory_space=pl.ANY)
    return pl.pallas_call(
        functools.partial(_swiglu_kernel, layer=layer, n_col_tiles=F // bn),
        grid=(F // bn, n_row_tiles),
        in_specs=[pl.BlockSpec((bm, K), lambda j, i: (i, 0)), any_spec, any_spec, any_spec],
        out_specs=[pl.BlockSpec((bm, bn), lambda j, i: (i, j)),
                   pl.BlockSpec((bn, D), lambda j, i: (j, 0))],
        out_shape=[jax.ShapeDtypeStruct((M, F), BF16), jax.ShapeDtypeStruct((F, D), BF16)],
        scratch_shapes=(_weight_stream_scratch(K, bn, n_row_tiles) + _weight_stream_scratch(K, bn, n_row_tiles)
                        + [pltpu.VMEM((2, bn // n_row_tiles, D), F32), pltpu.SemaphoreType.DMA((2,))]),
        compiler_params=_cparams("arbitrary", "arbitrary"),
        name="swiglu_up",
    )(h, wg_all, wu_all, wd_all)


def _merge_kernel(ya_ref, yb_ref, yc_ref, wa_hbm, wb_hbm, wc_hbm, ga_ref, gb_ref, gc_ref, o_ref,
                  wa16_ref, wa_stage, wa_sem, wb16_ref, wb_stage, wb_sem, wc16_ref, wc_stage, wc_sem, *,
                  layer, n_col_tiles):
    _stream_weight_tiles(wa_hbm, wa16_ref, wa_stage, wa_sem, layer=layer, n_col_tiles=n_col_tiles)
    _stream_weight_tiles(wb_hbm, wb16_ref, wb_stage, wb_sem, layer=layer, n_col_tiles=n_col_tiles)
    _stream_weight_tiles(wc_hbm, wc16_ref, wc_stage, wc_sem, layer=layer, n_col_tiles=n_col_tiles)
    slot = pl.program_id(0) % 2
    acc = ga_ref[...].astype(F32) * jnp.dot(ya_ref[...], wa16_ref[slot], preferred_element_type=F32)
    acc += gb_ref[...].astype(F32) * jnp.dot(yb_ref[...], wb16_ref[slot], preferred_element_type=F32)
    acc += gc_ref[...].astype(F32) * jnp.dot(yc_ref[...], wc16_ref[slot], preferred_element_type=F32)
    o_ref[...] = acc.astype(o_ref.dtype)


def gated_merge(ya, yb, yc, wa_all, wb_all, wc_all, layer, gates, d_model, *, bm, bn):
    M, Kb = ya.shape
    per_branch = d_model // bn
    n_row_tiles = M // bm
    y_spec = pl.BlockSpec((bm, Kb), lambda j, i: (i, 0))
    any_spec = pl.BlockSpec(memory_space=pl.ANY)

    def g_spec(k):
        return pl.BlockSpec((bm, bn), lambda j, i: (i, k * per_branch + j))

    return pl.pallas_call(
        functools.partial(_merge_kernel, layer=layer, n_col_tiles=per_branch),
        grid=(per_branch, n_row_tiles),
        in_specs=[y_spec, y_spec, y_spec, any_spec, any_spec, any_spec, g_spec(0), g_spec(1), g_spec(2)],
        out_specs=pl.BlockSpec((bm, bn), lambda j, i: (i, j)),
        out_shape=jax.ShapeDtypeStruct((M, d_model), BF16),
        scratch_shapes=_weight_stream_scratch(Kb, bn, n_row_tiles) * 3,
        compiler_params=_cparams("arbitrary", "arbitrary"),
        name="gated_merge",
    )(ya, yb, yc, wa_all, wb_all, wc_all, gates, gates, gates)


def _hgrn_tables(chunk):
    levels = int(math.log2(chunk))
    assert 1 << levels == chunk and levels >= 3
    t = np.arange(chunk)[:, None]
    u = np.arange(chunk)[None, :]
    masks = [(t == u)]
    for l in range(levels):
        m = 1 << l
        lower_t = (t % (2 * m)) >= m
        upper_u = (u % (2 * m)) < m
        masks.append(lower_t & upper_u & (t // (2 * m) == u // (2 * m)))
    return (u <= t).astype(np.float32), np.stack([m.astype(np.float32) for m in masks], axis=0)


def _split3_f32(x):
    p1 = x.astype(BF16).astype(F32)
    r1 = x - p1
    p2 = r1.astype(BF16).astype(F32)
    return p1, p2, r1 - p2


def _dot_nt(a, b):
    return lax.dot_general(a, b, (((1,), (1,)), ((), ())), preferred_element_type=F32)


def _hgrn_level_decays(f, b, chunk):
    dk = f.shape[1]
    row = lax.broadcasted_iota(jnp.int32, f.shape, 0)
    f_prev = pltpu.roll(f, 1, axis=0)
    f_next = pltpu.roll(f, chunk - 1, axis=0)
    r4 = row % 4
    decays = [jnp.where(row % 2 == 1, f, 1.0),
              jnp.where(r4 == 0, f_next, jnp.where(r4 == 1, 1.0, jnp.where(r4 == 2, f, f * f_prev)))]
    m = 4
    while m < chunk:
        ref = jnp.concatenate(
            [jnp.broadcast_to(b[p * 2 * m + m - 1:p * 2 * m + m, :], (2 * m, dk)) for p in range(chunk // (2 * m))],
            axis=0)
        lower = (row % (2 * m)) >= m
        decays.append(jnp.exp(jnp.where(lower, b - ref, ref - b)))
        m *= 2
    return decays


def _hgrn_kernel(q_ref, f_ref, v_ref, g_ref, lb_ref, gain_ref, tri_ref, masks_ref, o_ref, *,
                 chunk, n_chunks, unroll):
    dk = q_ref.shape[1]
    lb = lb_ref[...]
    gain = gain_ref[...]
    tri = tri_ref[...].astype(BF16)

    def body(c, st):
        rows = pl.ds(pl.multiple_of(c * chunk, chunk), chunk)
        q = q_ref[rows, :] * (HG_DK ** -0.5)
        f = lb + (1.0 - lb) * jax.nn.sigmoid(f_ref[rows, :])
        kk = 1.0 - f
        v16 = v_ref[rows, :].astype(BF16)
        parts = jnp.concatenate([p.astype(BF16) for p in _split3_f32(jnp.log(f))], axis=1)
        b3 = jnp.dot(tri, parts, preferred_element_type=F32)
        b = b3[:, :dk] + b3[:, dk:2 * dk] + b3[:, 2 * dk:]
        b_last = b[chunk - 1:chunk, :]
        q_inter = (q * jnp.exp(b)).astype(BF16)
        k_inter = (kk * jnp.exp(b_last - b)).astype(BF16)
        scores = masks_ref[0] * _dot_nt(q.astype(BF16), kk.astype(BF16))
        for l, xl in enumerate(_hgrn_level_decays(f, b, chunk)):
            scores += masks_ref[1 + l] * _dot_nt((q * xl).astype(BF16), (kk * xl).astype(BF16))
        o = jnp.dot(scores.astype(BF16), v16, preferred_element_type=F32)
        o += _dot_nt(q_inter, st.astype(BF16))
        upd = lax.dot_general(v16, k_inter, (((0,), (0,)), ((), ())), preferred_element_type=F32)
        o = o * lax.rsqrt(jnp.mean(o * o, axis=-1, keepdims=True) + NORM_EPS) * gain
        gate = g_ref[rows, :]
        o_ref[rows, :] = (o * (gate * jax.nn.sigmoid(gate))).astype(o_ref.dtype)
        return st * jnp.exp(b_last) + upd

    lax.fori_loop(0, n_chunks, body, jnp.zeros((HG_DV, dk), F32), unroll=unroll)


def hgrn2_mixer(proj, lb, out_gain, batch, seq, *, chunk=256, unroll=2):
    T = proj.shape[0]
    tri, masks = _hgrn_tables(chunk)

    def col_spec(off):
        return pl.BlockSpec((seq, HG_DK), lambda b, h: (b, off // HG_DK + h))

    vec_spec = pl.BlockSpec((1, HG_DK), lambda b, h: (0, h))
    return pl.pallas_call(
        functools.partial(_hgrn_kernel, chunk=chunk, n_chunks=seq // chunk, unroll=unroll),
        grid=(batch, HG_HEADS),
        in_specs=[col_spec(OFF_HQ), col_spec(OFF_HF), col_spec(OFF_HV), col_spec(OFF_HG),
                  vec_spec, vec_spec,
                  pl.BlockSpec(tri.shape, lambda b, h: (0, 0)),
                  pl.BlockSpec(masks.shape, lambda b, h: (0, 0, 0))],
        out_specs=pl.BlockSpec((seq, HG_DV), lambda b, h: (b, h)),
        out_shape=jax.ShapeDtypeStruct((T, HG_WIDTH), BF16),
        compiler_params=_cparams("parallel", "parallel"),
        name="hgrn2",
    )(proj, proj, proj, proj, lb.reshape(1, -1), out_gain.reshape(1, -1),
      jnp.asarray(tri), jnp.asarray(masks))


def _pool_kernel(u_ref, w_ref, scale_ref, o_ref):
    seq = u_ref.shape[0]
    row = lax.broadcasted_iota(jnp.int32, (seq, POOL_GROUP_DIM), 0)
    for j, window in enumerate(POOL_WINDOWS):
        cols = slice(j * POOL_GROUP_DIM, (j + 1) * POOL_GROUP_DIM)
        u = u_ref[:, cols]
        acc = u
        span = 1
        while span < window:
            shifted = jnp.where(row >= span, pltpu.roll(acc, span, axis=0), 0.0)
            acc = acc + shifted
            span *= 2
        count = jnp.minimum(row + 1, window).astype(F32)
        pooled = acc / count - u
        mixed = jnp.dot(pooled.astype(BF16), w_ref[j].astype(BF16), preferred_element_type=F32)
        o_ref[:, cols] = (mixed * scale_ref[:, cols]).astype(o_ref.dtype)


def pool_mixer(proj, w_groups, scale, batch, seq):
    T = proj.shape[0]
    assert all(w & (w - 1) == 0 for w in POOL_WINDOWS)
    return pl.pallas_call(
        _pool_kernel,
        grid=(batch,),
        in_specs=[pl.BlockSpec((seq, POOL_WIDTH), lambda b: (b, OFF_PU // POOL_WIDTH)),
                  pl.BlockSpec(w_groups.shape, lambda b: (0, 0, 0)),
                  pl.BlockSpec((1, POOL_WIDTH), lambda b: (0, 0))],
        out_specs=pl.BlockSpec((seq, POOL_WIDTH), lambda b: (b, 0)),
        out_shape=jax.ShapeDtypeStruct((T, POOL_WIDTH), BF16),
        compiler_params=_cparams("parallel"),
        name="pool",
    )(proj, w_groups, scale.reshape(1, -1))


DA_AUG = 2 * DA_VDIM


def _diffattn_kernel(slopes_ref, q_ref, k_ref, v_ref, lam_ref, subln_ref, o_ref, ka_ref, va_ref, s_ref, *,
                     seq, blk, lambda_init):
    slope = slopes_ref[pl.program_id(1)]
    lp = lam_ref[...]
    lam = (jnp.exp(jnp.sum(lp[0:1] * lp[1:2], axis=-1, keepdims=True))
           - jnp.exp(jnp.sum(lp[2:3] * lp[3:4], axis=-1, keepdims=True)) + lambda_init)

    lane = lax.broadcasted_iota(jnp.int32, (seq, DA_VDIM), 1)
    kpos = lax.broadcasted_iota(jnp.int32, (seq, DA_VDIM), 0).astype(F32) * (slope * LOG2E)
    p1, p2, p3 = _split3_f32(kpos)
    ka_ref[:, 0:DA_VDIM] = k_ref[...].astype(BF16)
    ka_ref[:, DA_VDIM:DA_AUG] = jnp.where(lane == 0, p1, jnp.where(lane == 1, p2, jnp.where(lane == 2, p3, 0.0))
                                          ).astype(BF16)
    va_ref[:, 0:DA_VDIM] = v_ref[...].astype(BF16)
    va_ref[:, DA_VDIM:DA_AUG] = jnp.where(lane == 0, 1.0, 0.0).astype(BF16)

    qlane = lax.broadcasted_iota(jnp.int32, (blk, DA_VDIM), 1)
    ones3 = jnp.where(qlane < 3, 1.0, 0.0).astype(BF16)
    key_gt_query = (lax.broadcasted_iota(jnp.int32, (2 * blk, blk), 1)
                    > lax.broadcasted_iota(jnp.int32, (2 * blk, blk), 0) % blk)

    for i in range(seq // blk):
        rows = pl.ds(i * blk, blk)
        q = q_ref[rows, :] * (DA_HEAD_DIM ** -0.5 * LOG2E)
        q2 = jnp.concatenate(
            [jnp.concatenate([jnp.where(qlane < DA_HEAD_DIM, q, 0.0).astype(BF16), ones3], axis=1),
             jnp.concatenate([jnp.where(qlane >= DA_HEAD_DIM, q, 0.0).astype(BF16), ones3], axis=1)],
            axis=0)
        m_tile = jnp.full((2 * blk, DA_VDIM), -jnp.inf, F32)
        for j in range(i + 1):
            keys = pl.ds(j * blk, blk)
            s = _dot_nt(q2, ka_ref[keys, :])
            if j == i:
                s = jnp.where(key_gt_query, -jnp.inf, s)
            s_ref[:, keys] = s
            for c in range(blk // DA_VDIM):
                m_tile = jnp.maximum(m_tile, s[:, c * DA_VDIM:(c + 1) * DA_VDIM])
        m_row = jnp.max(m_tile, axis=-1, keepdims=True)
        acc = jnp.zeros((2 * blk, DA_AUG), F32)
        for j in range(i + 1):
            keys = pl.ds(j * blk, blk)
            p = jnp.exp2(s_ref[:, keys] - m_row).astype(BF16)
            acc += jnp.dot(p, va_ref[keys, :], preferred_element_type=F32)
        out0 = acc[0:blk, 0:DA_VDIM] * (1.0 / acc[0:blk, DA_VDIM:DA_VDIM + 1])
        out1 = acc[blk:2 * blk, 0:DA_VDIM] * (1.0 / acc[blk:2 * blk, DA_VDIM:DA_VDIM + 1])
        o = out0 - lam * out1
        o = o * lax.rsqrt(jnp.mean(o * o, axis=-1, keepdims=True) + NORM_EPS) * subln_ref[...]
        o_ref[rows, :] = (o * (1.0 - lambda_init)).astype(o_ref.dtype)


def diff_attention(proj, lam_params, subln, lambda_init, batch, seq, *, blk=256):
    T = proj.shape[0]
    slopes = (2.0 ** (-8.0 * jnp.arange(1, DA_HEADS + 1, dtype=F32) / DA_HEADS)).astype(F32)

    def col_spec(off):
        return pl.BlockSpec((seq, DA_VDIM), lambda b, h, s: (b, off // DA_VDIM + h))

    grid_spec = pltpu.PrefetchScalarGridSpec(
        num_scalar_prefetch=1,
        grid=(batch, DA_HEADS),
        in_specs=[col_spec(OFF_DQ), col_spec(OFF_DK), col_spec(OFF_DV),
                  pl.BlockSpec(lam_params.shape, lambda b, h, s: (0, 0)),
                  pl.BlockSpec((1, DA_VDIM), lambda b, h, s: (0, 0))],
        out_specs=pl.BlockSpec((seq, DA_VDIM), lambda b, h, s: (b, h)),
        scratch_shapes=[pltpu.VMEM((seq, DA_AUG), BF16), pltpu.VMEM((seq, DA_AUG), BF16),
                        pltpu.VMEM((2 * blk, seq), F32)],
    )
    return pl.pallas_call(
        functools.partial(_diffattn_kernel, seq=seq, blk=blk, lambda_init=lambda_init),
        grid_spec=grid_spec,
        out_shape=jax.ShapeDtypeStruct((T, DA_WIDTH), BF16),
        compiler_params=_cparams("parallel", "parallel"),
        name="diffattn",
    )(slopes, proj, proj, proj, lam_params, subln.reshape(1, -1))


def kernel(x, norm_mix_pre, norm_mix_post, norm_ffn_pre, norm_ffn_post, w_in, hgrn_lb_logits, hgrn_out_norm,
           pool_w, pool_scale, diff_lambda, diff_subln, w_up_a, w_up_b, w_up_c, w_out, w_ffn_gate, w_ffn_up,
           w_ffn_down):
    B, S, D = x.shape
    depth = w_in.shape[0]
    T = B * S
    lb_all = jnp.cumsum(jax.nn.softmax(hgrn_lb_logits.astype(F32), axis=0), axis=0)
    lb_all = lb_all - lb_all[0:1]

    x2 = x.reshape(T, D)
    h = prenorm(x2, norm_mix_pre[0])
    for l in range(depth):
        lambda_init = 0.8 - 0.6 * math.exp(-0.3 * l)
        proj = matmul_ws(h, w_in, l, bm=1024, bn=1024, out_dtype=F32, name="in_proj_mix", n_cols=OFF_GATE)
        gates = matmul_ws(h, w_in, l, bm=1024, bn=1024, out_dtype=BF16, name="in_proj_gate", col0=OFF_GATE,
                          sigmoid_out=True)
        y_a = hgrn2_mixer(proj, lb_all[l], hgrn_out_norm[l], B, S)
        y_b = pool_mixer(proj, pool_w[l], pool_scale[l], B, S)
        y_c = diff_attention(proj, diff_lambda[l], diff_subln[l], lambda_init, B, S)
        merged = gated_merge(y_a, y_b, y_c, w_up_a, w_up_b, w_up_c, l, gates, D, bm=1024, bn=1024)
        z = matmul_ws(merged, w_out, l, bm=1024, bn=1024, out_dtype=F32, name="out_proj")
        x2, h = postnorm(x2, z, norm_mix_post[l], norm_ffn_pre[l])
        u, wd16 = swiglu_up(h, w_ffn_gate, w_ffn_up, w_ffn_down, l, bm=2048, bn=256)
        ff = matmul_bf16(u, wd16, bm=512, bn=512, out_dtype=F32, name="ffn_down")
        gnext = norm_mix_pre[l + 1] if l + 1 < depth else None
        x2, h = postnorm(x2, ff, norm_ffn_post[l], gnext)
    return x2.reshape(B, S, D)
```

```python
import functools
import math

import jax
import jax.numpy as jnp
import numpy as np
from jax import lax
from jax.experimental import pallas as pl
from jax.experimental.pallas import tpu as pltpu

HG_HEADS = 8
HG_DK = 128
HG_DV = 128
HG_WIDTH = HG_HEADS * HG_DV
HG_CHUNK = 64
POOL_WINDOWS = (2, 4, 8, 16)
POOL_GROUPS = 4
POOL_GROUP_DIM = 256
POOL_WIDTH = POOL_GROUPS * POOL_GROUP_DIM
DA_HEADS = 8
DA_HEAD_DIM = 64
DA_VDIM = 2 * DA_HEAD_DIM
DA_WIDTH = DA_HEADS * DA_VDIM
N_BRANCH = 3
NORM_EPS = 1e-6
LOG2E = 1.4426950408889634

OFF_HQ = 0
OFF_HF = OFF_HQ + HG_HEADS * HG_DK
OFF_HV = OFF_HF + HG_HEADS * HG_DK
OFF_HG = OFF_HV + HG_WIDTH
OFF_PU = OFF_HG + HG_WIDTH
OFF_DQ = OFF_PU + POOL_WIDTH
OFF_DK = OFF_DQ + DA_WIDTH
OFF_DV = OFF_DK + DA_WIDTH
OFF_GATE = OFF_DV + DA_WIDTH

SUBLANES = 8
V7X_VMEM_LIMIT_BYTES = 56 * 1024 * 1024

BF16 = jnp.bfloat16
F32 = jnp.float32


def _cparams(*sem):
    return pltpu.CompilerParams(dimension_semantics=sem, vmem_limit_bytes=V7X_VMEM_LIMIT_BYTES)


def _rms(x, gain):
    return x * lax.rsqrt(jnp.mean(x * x, axis=-1, keepdims=True) + NORM_EPS) * gain


def _prenorm_kernel(x_ref, g_ref, h_ref):
    h_ref[...] = _rms(x_ref[...], g_ref[...]).astype(h_ref.dtype)


def prenorm(x2, gain, *, rows=256):
    T, D = x2.shape
    return pl.pallas_call(
        _prenorm_kernel,
        grid=(T // rows,),
        in_specs=[pl.BlockSpec((rows, D), lambda i: (i, 0)),
                  pl.BlockSpec((1, D), lambda i: (0, 0))],
        out_specs=pl.BlockSpec((rows, D), lambda i: (i, 0)),
        out_shape=jax.ShapeDtypeStruct((T, D), BF16),
        compiler_params=_cparams("parallel"),
        name="prenorm",
    )(x2, gain.reshape(1, D))


def _postnorm_kernel(x_ref, z_ref, gpost_ref, gnext_ref, xo_ref, h_ref):
    xn = x_ref[...] + _rms(z_ref[...].astype(F32), gpost_ref[...])
    xo_ref[...] = xn
    h_ref[...] = _rms(xn, gnext_ref[...]).astype(h_ref.dtype)


def _postnorm_last_kernel(x_ref, z_ref, gpost_ref, xo_ref):
    xo_ref[...] = x_ref[...] + _rms(z_ref[...].astype(F32), gpost_ref[...])


def postnorm(x2, z, gpost, gnext, *, rows=256):
    T, D = x2.shape
    row_spec = pl.BlockSpec((rows, D), lambda i: (i, 0))
    vec_spec = pl.BlockSpec((1, D), lambda i: (0, 0))
    if gnext is None:
        return pl.pallas_call(
            _postnorm_last_kernel,
            grid=(T // rows,),
            in_specs=[row_spec, row_spec, vec_spec],
            out_specs=row_spec,
            out_shape=jax.ShapeDtypeStruct((T, D), F32),
            compiler_params=_cparams("parallel"),
            name="postnorm_last",
        )(x2, z, gpost.reshape(1, D)), None
    return pl.pallas_call(
        _postnorm_kernel,
        grid=(T // rows,),
        in_specs=[row_spec, row_spec, vec_spec, vec_spec],
        out_specs=[row_spec, row_spec],
        out_shape=[jax.ShapeDtypeStruct((T, D), F32), jax.ShapeDtypeStruct((T, D), BF16)],
        compiler_params=_cparams("parallel"),
        name="postnorm",
    )(x2, z, gpost.reshape(1, D), gnext.reshape(1, D))


def _stream_weight_tiles(w_hbm, w16_ref, stage_ref, sem, *, layer, n_col_tiles, col0=0):
    j, i = pl.program_id(0), pl.program_id(1)
    _, k_total, bn = w16_ref.shape
    kc = stage_ref.shape[1]
    n_chunks = k_total // kc
    g = j * n_chunks + i
    n_steps = n_col_tiles * n_chunks

    def copy(tile, chunk, slot):
        return pltpu.make_async_copy(w_hbm.at[layer, pl.ds(chunk * kc, kc), pl.ds(col0 + tile * bn, bn)],
                                     stage_ref.at[slot], sem.at[slot])

    def stream_tile(step):
        return (1 + step // n_chunks) % n_col_tiles

    @pl.when(g == 0)
    def _first_tile():
        copy(0, 0, 0).start()
        for c in range(n_chunks):
            if c + 1 < n_chunks:
                copy(0, c + 1, (c + 1) % 2).start()
            copy(0, c, c % 2).wait()
            w16_ref[0, c * kc:(c + 1) * kc, :] = stage_ref[c % 2].astype(BF16)
        copy(stream_tile(0), 0, 0).start()

    @pl.when(g + 1 < n_steps)
    def _():
        copy(stream_tile(g + 1), (g + 1) % n_chunks, (g + 1) % 2).start()

    copy(stream_tile(g), i, g % 2).wait()
    w16_ref[(j + 1) % 2, pl.ds(pl.multiple_of(i * kc, kc), kc), :] = stage_ref[g % 2].astype(BF16)


def _mm_ws_kernel(a_ref, w_hbm, o_ref, w16_ref, stage_ref, sem, *, layer, n_col_tiles, col0, sigmoid_out):
    _stream_weight_tiles(w_hbm, w16_ref, stage_ref, sem, layer=layer, n_col_tiles=n_col_tiles, col0=col0)
    w16 = w16_ref[pl.program_id(0) % 2]
    acc = jnp.dot(a_ref[...], w16, preferred_element_type=F32)
    if sigmoid_out:
        acc = jax.nn.sigmoid(acc)
    o_ref[...] = acc.astype(o_ref.dtype)


def _weight_stream_scratch(k_total, bn, n_row_tiles):
    assert k_total % n_row_tiles == 0 and n_row_tiles % 2 == 0
    return [pltpu.VMEM((2, k_total, bn), BF16), pltpu.VMEM((2, k_total // n_row_tiles, bn), F32),
            pltpu.SemaphoreType.DMA((2,))]


def matmul_ws(a, w_all, layer, *, bm, bn, out_dtype, name, col0=0, n_cols=None, sigmoid_out=False):
    M, K = a.shape
    N = w_all.shape[2] - col0 if n_cols is None else n_cols
    assert N % bn == 0 and M % bm == 0
    return pl.pallas_call(
        functools.partial(_mm_ws_kernel, layer=layer, n_col_tiles=N // bn, col0=col0, sigmoid_out=sigmoid_out),
        grid=(N // bn, M // bm),
        in_specs=[pl.BlockSpec((bm, K), lambda j, i: (i, 0)),
                  pl.BlockSpec(memory_space=pl.ANY)],
        out_specs=pl.BlockSpec((bm, bn), lambda j, i: (i, j)),
        out_shape=jax.ShapeDtypeStruct((M, N), out_dtype),
        scratch_shapes=_weight_stream_scratch(K, bn, M // bm),
        compiler_params=_cparams("arbitrary", "arbitrary"),
        name=name,
    )(a, w_all)


def _mm_ksplit_kernel(a_ref, w_ref, o_ref, acc_ref):
    k, j = pl.program_id(1), pl.program_id(2)
    last = pl.num_programs(1) - 1
    part = jnp.dot(a_ref[...], w_ref[...], preferred_element_type=F32)

    @pl.when(k == 0)
    def _():
        acc_ref[j] = part

    @pl.when((k > 0) & (k < last))
    def _():
        acc_ref[j] += part

    @pl.when(k == last)
    def _():
        o_ref[...] = (acc_ref[j] + part).astype(o_ref.dtype)


def matmul_bf16_ksplit(a, w, *, bm, bn, nk, out_dtype, name):
    M, K = a.shape
    N = w.shape[1]
    assert K % nk == 0 and nk >= 2
    last = nk - 1
    return pl.pallas_call(
        _mm_ksplit_kernel,
        grid=(M // bm, nk, N // bn),
        in_specs=[pl.BlockSpec((bm, K // nk), lambda i, k, j: (i, k)),
                  pl.BlockSpec((K // nk, bn), lambda i, k, j: (k, j))],
        out_specs=pl.BlockSpec((bm, bn), lambda i, k, j: (i, jnp.where(k == last, j, 0))),
        out_shape=jax.ShapeDtypeStruct((M, N), out_dtype),
        scratch_shapes=[pltpu.VMEM((N // bn, bm, bn), F32)],
        compiler_params=_cparams("arbitrary", "arbitrary", "arbitrary"),
        name=name,
    )(a, w)


def _cast_rows_stream(src_hbm, out_ref, stage_ref, sem, *, layer):
    j, i = pl.program_id(0), pl.program_id(1)
    n_row_tiles = pl.num_programs(1)
    g = j * n_row_tiles + i
    n_steps = pl.num_programs(0) * n_row_tiles
    r = stage_ref.shape[1]

    def copy(step, slot):
        return pltpu.make_async_copy(src_hbm.at[layer, pl.ds(step * r, r), :], stage_ref.at[slot], sem.at[slot])

    @pl.when(g == 0)
    def _():
        copy(0, 0).start()

    @pl.when(g + 1 < n_steps)
    def _():
        copy(g + 1, (g + 1) % 2).start()

    copy(g, g % 2).wait()
    out_ref[pl.ds(pl.multiple_of(i * r, r), r), :] = stage_ref[g % 2].astype(BF16)


def _swiglu_kernel(h_ref, wg_hbm, wu_hbm, wd_hbm, o_ref, wd16_ref,
                   wg16_ref, wg_stage, wg_sem, wu16_ref, wu_stage, wu_sem, wd_stage, wd_sem, *,
                   layer, n_col_tiles):
    _stream_weight_tiles(wg_hbm, wg16_ref, wg_stage, wg_sem, layer=layer, n_col_tiles=n_col_tiles)
    _stream_weight_tiles(wu_hbm, wu16_ref, wu_stage, wu_sem, layer=layer, n_col_tiles=n_col_tiles)
    _cast_rows_stream(wd_hbm, wd16_ref, wd_stage, wd_sem, layer=layer)
    slot = pl.program_id(0) % 2
    h = h_ref[...]
    a = jnp.dot(h, wg16_ref[slot], preferred_element_type=F32)
    b = jnp.dot(h, wu16_ref[slot], preferred_element_type=F32)
    o_ref[...] = (a * jax.nn.sigmoid(a) * b).astype(o_ref.dtype)


def swiglu_up(h, wg_all, wu_all, wd_all, layer, *, bm, bn):
    M, K = h.shape
    F = wg_all.shape[2]
    D = wd_all.shape[2]
    n_row_tiles = M // bm
    assert F % bn == 0 and bn % n_row_tiles == 0
    any_spec = pl.BlockSpec(memory_space=pl.ANY)
    return pl.pallas_call(
        functools.partial(_swiglu_kernel, layer=layer, n_col_tiles=F // bn),
        grid=(F // bn, n_row_tiles),
        in_specs=[pl.BlockSpec((bm, K), lambda j, i: (i, 0)), any_spec, any_spec, any_spec],
        out_specs=[pl.BlockSpec((bm, bn), lambda j, i: (i, j)),
                   pl.BlockSpec((bn, D), lambda j, i: (j, 0))],
        out_shape=[jax.ShapeDtypeStruct((M, F), BF16), jax.ShapeDtypeStruct((F, D), BF16)],
        scratch_shapes=(_weight_stream_scratch(K, bn, n_row_tiles) + _weight_stream_scratch(K, bn, n_row_tiles)
                        + [pltpu.VMEM((2, bn // n_row_tiles, D), F32), pltpu.SemaphoreType.DMA((2,))]),
        compiler_params=_cparams("arbitrary", "arbitrary"),
        name="swiglu_up",
    )(h, wg_all, wu_all, wd_all)


def _merge_kernel(ya_ref, yb_ref, yc_ref, wa_hbm, wb_hbm, wc_hbm, ga_ref, gb_ref, gc_ref, o_ref,
                  wa16_ref, wa_stage, wa_sem, wb16_ref, wb_stage, wb_sem, wc16_ref, wc_stage, wc_sem, *,
                  layer, n_col_tiles):
    _stream_weight_tiles(wa_hbm, wa16_ref, wa_stage, wa_sem, layer=layer, n_col_tiles=n_col_tiles)
    _stream_weight_tiles(wb_hbm, wb16_ref, wb_stage, wb_sem, layer=layer, n_col_tiles=n_col_tiles)
    _stream_weight_tiles(wc_hbm, wc16_ref, wc_stage, wc_sem, layer=layer, n_col_tiles=n_col_tiles)
    slot = pl.program_id(0) % 2
    acc = ga_ref[...].astype(F32) * jnp.dot(ya_ref[...], wa16_ref[slot], preferred_element_type=F32)
    acc += gb_ref[...].astype(F32) * jnp.dot(yb_ref[...], wb16_ref[slot], preferred_element_type=F32)
    acc += gc_ref[...].astype(F32) * jnp.dot(yc_ref[...], wc16_ref[slot], preferred_element_type=F32)
    o_ref[...] = acc.astype(o_ref.dtype)


def gated_merge(ya, yb, yc, wa_all, wb_all, wc_all, layer, gates, d_model, *, bm, bn):
    M, Kb = ya.shape
    per_branch = d_model // bn
    n_row_tiles = M // bm
    y_spec = pl.BlockSpec((bm, Kb), lambda j, i: (i, 0))
    any_spec = pl.BlockSpec(memory_space=pl.ANY)

    def g_spec(k):
        return pl.BlockSpec((bm, bn), lambda j, i: (i, k * per_branch + j))

    return pl.pallas_call(
        functools.partial(_merge_kernel, layer=layer, n_col_tiles=per_branch),
        grid=(per_branch, n_row_tiles),
        in_specs=[y_spec, y_spec, y_spec, any_spec, any_spec, any_spec, g_spec(0), g_spec(1), g_spec(2)],
        out_specs=pl.BlockSpec((bm, bn), lambda j, i: (i, j)),
        out_shape=jax.ShapeDtypeStruct((M, d_model), BF16),
        scratch_shapes=_weight_stream_scratch(Kb, bn, n_row_tiles) * 3,
        compiler_params=_cparams("arbitrary", "arbitrary"),
        name="gated_merge",
    )(ya, yb, yc, wa_all, wb_all, wc_all, gates, gates, gates)


def _hgrn_tables(chunk):
    levels = int(math.log2(chunk))
    assert 1 << levels == chunk and levels >= 3
    t = np.arange(chunk)[:, None]
    u = np.arange(chunk)[None, :]
    masks = [(t == u)]
    for l in range(levels):
        m = 1 << l
        lower_t = (t % (2 * m)) >= m
        upper_u = (u % (2 * m)) < m
        masks.append(lower_t & upper_u & (t // (2 * m) == u // (2 * m)))
    return (u <= t).astype(np.float32), np.stack([m.astype(np.float32) for m in masks], axis=0)


def _split3_f32(x):
    p1 = x.astype(BF16).astype(F32)
    r1 = x - p1
    p2 = r1.astype(BF16).astype(F32)
    return p1, p2, r1 - p2


def _dot_nt(a, b):
    return lax.dot_general(a, b, (((1,), (1,)), ((), ())), preferred_element_type=F32)


def _hgrn_level_decays(f, b, chunk):
    dk = f.shape[1]
    row = lax.broadcasted_iota(jnp.int32, f.shape, 0)
    f_prev = pltpu.roll(f, 1, axis=0)
    f_next = pltpu.roll(f, chunk - 1, axis=0)
    r4 = row % 4
    decays = [jnp.where(row % 2 == 1, f, 1.0),
              jnp.where(r4 == 0, f_next, jnp.where(r4 == 1, 1.0, jnp.where(r4 == 2, f, f * f_prev)))]
    m = 4
    while m < chunk:
        ref = jnp.concatenate(
            [jnp.broadcast_to(b[p * 2 * m + m - 1:p * 2 * m + m, :], (2 * m, dk)) for p in range(chunk // (2 * m))],
            axis=0)
        lower = (row % (2 * m)) >= m
        decays.append(jnp.exp(jnp.where(lower, b - ref, ref - b)))
        m *= 2
    return decays


def _hgrn_kernel(q_ref, f_ref, v_ref, g_ref, lb_ref, gain_ref, tri_ref, masks_ref, o_ref, *,
                 chunk, n_chunks, unroll):
    dk = q_ref.shape[1]
    lb = lb_ref[...]
    gain = gain_ref[...]
    tri = tri_ref[...].astype(BF16)

    def body(c, st):
        rows = pl.ds(pl.multiple_of(c * chunk, chunk), chunk)
        q = q_ref[rows, :] * (HG_DK ** -0.5)
        f = lb + (1.0 - lb) * jax.nn.sigmoid(f_ref[rows, :])
        kk = 1.0 - f
        v16 = v_ref[rows, :].astype(BF16)
        parts = jnp.concatenate([p.astype(BF16) for p in _split3_f32(jnp.log(f))], axis=1)
        b3 = jnp.dot(tri, parts, preferred_element_type=F32)
        b = b3[:, :dk] + b3[:, dk:2 * dk] + b3[:, 2 * dk:]
        b_last = b[chunk - 1:chunk, :]
        q_inter = (q * jnp.exp(b)).astype(BF16)
        k_inter = (kk * jnp.exp(b_last - b)).astype(BF16)
        scores = masks_ref[0] * _dot_nt(q.astype(BF16), kk.astype(BF16))
        for l, xl in enumerate(_hgrn_level_decays(f, b, chunk)):
            scores += masks_ref[1 + l] * _dot_nt((q * xl).astype(BF16), (kk * xl).astype(BF16))
        o = jnp.dot(scores.astype(BF16), v16, preferred_element_type=F32)
        o += _dot_nt(q_inter, st.astype(BF16))
        upd = lax.dot_general(v16, k_inter, (((0,), (0,)), ((), ())), preferred_element_type=F32)
        o = o * lax.rsqrt(jnp.mean(o * o, axis=-1, keepdims=True) + NORM_EPS) * gain
        gate = g_ref[rows, :]
        o_ref[rows, :] = (o * (gate * jax.nn.sigmoid(gate))).astype(o_ref.dtype)
        return st * jnp.exp(b_last) + upd

    lax.fori_loop(0, n_chunks, body, jnp.zeros((HG_DV, dk), F32), unroll=unroll)


def hgrn2_mixer(proj, lb, out_gain, batch, seq, *, chunk=256, unroll=2):
    T = proj.shape[0]
    tri, masks = _hgrn_tables(chunk)

    def col_spec(off):
        return pl.BlockSpec((seq, HG_DK), lambda b, h: (b, off // HG_DK + h))

    vec_spec = pl.BlockSpec((1, HG_DK), lambda b, h: (0, h))
    return pl.pallas_call(
        functools.partial(_hgrn_kernel, chunk=chunk, n_chunks=seq // chunk, unroll=unroll),
        grid=(batch, HG_HEADS),
        in_specs=[col_spec(OFF_HQ), col_spec(OFF_HF), col_spec(OFF_HV), col_spec(OFF_HG),
                  vec_spec, vec_spec,
                  pl.BlockSpec(tri.shape, lambda b, h: (0, 0)),
                  pl.BlockSpec(masks.shape, lambda b, h: (0, 0, 0))],
        out_specs=pl.BlockSpec((seq, HG_DV), lambda b, h: (b, h)),
        out_shape=jax.ShapeDtypeStruct((T, HG_WIDTH), BF16),
        compiler_params=_cparams("parallel", "parallel"),
        name="hgrn2",
    )(proj, proj, proj, proj, lb.reshape(1, -1), out_gain.reshape(1, -1),
      jnp.asarray(tri), jnp.asarray(masks))


def _pool_kernel(u_ref, w_ref, scale_ref, o_ref):
    seq = u_ref.shape[0]
    row = lax.broadcasted_iota(jnp.int32, (seq, POOL_GROUP_DIM), 0)
    for j, window in enumerate(POOL_WINDOWS):
        cols = slice(j * POOL_GROUP_DIM, (j + 1) * POOL_GROUP_DIM)
        u = u_ref[:, cols]
        acc = u
        span = 1
        while span < window:
            shifted = jnp.where(row >= span, pltpu.roll(acc, span, axis=0), 0.0)
            acc = acc + shifted
            span *= 2
        count = jnp.minimum(row + 1, window).astype(F32)
        pooled = acc / count - u
        mixed = jnp.dot(pooled.astype(BF16), w_ref[j].astype(BF16), preferred_element_type=F32)
        o_ref[:, cols] = (mixed * scale_ref[:, cols]).astype(o_ref.dtype)


def pool_mixer(proj, w_groups, scale, batch, seq):
    T = proj.shape[0]
    assert all(w & (w - 1) == 0 for w in POOL_WINDOWS)
    return pl.pallas_call(
        _pool_kernel,
        grid=(batch,),
        in_specs=[pl.BlockSpec((seq, POOL_WIDTH), lambda b: (b, OFF_PU // POOL_WIDTH)),
                  pl.BlockSpec(w_groups.shape, lambda b: (0, 0, 0)),
                  pl.BlockSpec((1, POOL_WIDTH), lambda b: (0, 0))],
        out_specs=pl.BlockSpec((seq, POOL_WIDTH), lambda b: (b, 0)),
        out_shape=jax.ShapeDtypeStruct((T, POOL_WIDTH), BF16),
        compiler_params=_cparams("parallel"),
        name="pool",
    )(proj, w_groups, scale.reshape(1, -1))


DA_AUG = 2 * DA_VDIM


def _diffattn_kernel(slopes_ref, q_ref, k_ref, v_ref, lam_ref, subln_ref, o_ref, ka_ref, va_ref, s_ref, *,
                     seq, blk, lambda_init):
    slope = slopes_ref[pl.program_id(1)]
    lp = lam_ref[...]
    lam = (jnp.exp(jnp.sum(lp[0:1] * lp[1:2], axis=-1, keepdims=True))
           - jnp.exp(jnp.sum(lp[2:3] * lp[3:4], axis=-1, keepdims=True)) + lambda_init)

    lane = lax.broadcasted_iota(jnp.int32, (seq, DA_VDIM), 1)
    kpos = lax.broadcasted_iota(jnp.int32, (seq, DA_VDIM), 0).astype(F32) * (slope * LOG2E)
    p1, p2, p3 = _split3_f32(kpos)
    ka_ref[:, 0:DA_VDIM] = k_ref[...].astype(BF16)
    ka_ref[:, DA_VDIM:DA_AUG] = jnp.where(lane == 0, p1, jnp.where(lane == 1, p2, jnp.where(lane == 2, p3, 0.0))
                                          ).astype(BF16)
    va_ref[:, 0:DA_VDIM] = v_ref[...].astype(BF16)
    va_ref[:, DA_VDIM:DA_AUG] = jnp.where(lane == 0, 1.0, 0.0).astype(BF16)

    qlane = lax.broadcasted_iota(jnp.int32, (blk, DA_VDIM), 1)
    ones3 = jnp.where(qlane < 3, 1.0, 0.0).astype(BF16)
    key_gt_query = (lax.broadcasted_iota(jnp.int32, (2 * blk, blk), 1)
                    > lax.broadcasted_iota(jnp.int32, (2 * blk, blk), 0) % blk)

    for i in range(seq // blk):
        rows = pl.ds(i * blk, blk)
        q = q_ref[rows, :] * (DA_HEAD_DIM ** -0.5 * LOG2E)
        q2 = jnp.concatenate(
            [jnp.concatenate([jnp.where(qlane < DA_HEAD_DIM, q, 0.0).astype(BF16), ones3], axis=1),
             jnp.concatenate([jnp.where(qlane >= DA_HEAD_DIM, q, 0.0).astype(BF16), ones3], axis=1)],
            axis=0)
        m_tile = jnp.full((2 * blk, DA_VDIM), -jnp.inf, F32)
        for j in range(i + 1):
            keys = pl.ds(j * blk, blk)
            s = _dot_nt(q2, ka_ref[keys, :])
            if j == i:
                s = jnp.where(key_gt_query, -jnp.inf, s)
            s_ref[:, keys] = s
            for c in range(blk // DA_VDIM):
                m_tile = jnp.maximum(m_tile, s[:, c * DA_VDIM:(c + 1) * DA_VDIM])
        m_row = jnp.max(m_tile, axis=-1, keepdims=True)
        acc = jnp.zeros((2 * blk, DA_AUG), F32)
        for j in range(i + 1):
            keys = pl.ds(j * blk, blk)
            p = jnp.exp2(s_ref[:, keys] - m_row).astype(BF16)
            acc += jnp.dot(p, va_ref[keys, :], preferred_element_type=F32)
        out0 = acc[0:blk, 0:DA_VDIM] * (1.0 / acc[0:blk, DA_VDIM:DA_VDIM + 1])
        out1 = acc[blk:2 * blk, 0:DA_VDIM] * (1.0 / acc[blk:2 * blk, DA_VDIM:DA_VDIM + 1])
        o = out0 - lam * out1
        o = o * lax.rsqrt(jnp.mean(o * o, axis=-1, keepdims=True) + NORM_EPS) * subln_ref[...]
        o_ref[rows, :] = (o * (1.0 - lambda_init)).astype(o_ref.dtype)


def diff_attention(proj, lam_params, subln, lambda_init, batch, seq, *, blk=256):
    T = proj.shape[0]
    slopes = (2.0 ** (-8.0 * jnp.arange(1, DA_HEADS + 1, dtype=F32) / DA_HEADS)).astype(F32)

    def col_spec(off):
        return pl.BlockSpec((seq, DA_VDIM), lambda b, h, s: (b, off // DA_VDIM + h))

    grid_spec = pltpu.PrefetchScalarGridSpec(
        num_scalar_prefetch=1,
        grid=(batch, DA_HEADS),
        in_specs=[col_spec(OFF_DQ), col_spec(OFF_DK), col_spec(OFF_DV),
                  pl.BlockSpec(lam_params.shape, lambda b, h, s: (0, 0)),
                  pl.BlockSpec((1, DA_VDIM), lambda b, h, s: (0, 0))],
        out_specs=pl.BlockSpec((seq, DA_VDIM), lambda b, h, s: (b, h)),
        scratch_shapes=[pltpu.VMEM((seq, DA_AUG), BF16), pltpu.VMEM((seq, DA_AUG), BF16),
                        pltpu.VMEM((2 * blk, seq), F32)],
    )
    return pl.pallas_call(
        functools.partial(_diffattn_kernel, seq=seq, blk=blk, lambda_init=lambda_init),
        grid_spec=grid_spec,
        out_shape=jax.ShapeDtypeStruct((T, DA_WIDTH), BF16),
        compiler_params=_cparams("parallel", "parallel"),
        name="diffattn",
    )(slopes, proj, proj, proj, lam_params, subln.reshape(1, -1))


def kernel(x, norm_mix_pre, norm_mix_post, norm_ffn_pre, norm_ffn_post, w_in, hgrn_lb_logits, hgrn_out_norm,
           pool_w, pool_scale, diff_lambda, diff_subln, w_up_a, w_up_b, w_up_c, w_out, w_ffn_gate, w_ffn_up,
           w_ffn_down):
    B, S, D = x.shape
    depth = w_in.shape[0]
    T = B * S
    lb_all = jnp.cumsum(jax.nn.softmax(hgrn_lb_logits.astype(F32), axis=0), axis=0)
    lb_all = lb_all - lb_all[0:1]

    x2 = x.reshape(T, D)
    h = prenorm(x2, norm_mix_pre[0])
    for l in range(depth):
        lambda_init = 0.8 - 0.6 * math.exp(-0.3 * l)
        proj = matmul_ws(h, w_in, l, bm=1024, bn=1024, out_dtype=F32, name="in_proj_mix", n_cols=OFF_GATE)
        gates = matmul_ws(h, w_in, l, bm=1024, bn=1024, out_dtype=BF16, name="in_proj_gate", col0=OFF_GATE,
                          sigmoid_out=True)
        y_a = hgrn2_mixer(proj, lb_all[l], hgrn_out_norm[l], B, S)
        y_b = pool_mixer(proj, pool_w[l], pool_scale[l], B, S)
        y_c = diff_attention(proj, diff_lambda[l], diff_subln[l], lambda_init, B, S)
        merged = gated_merge(y_a, y_b, y_c, w_up_a, w_up_b, w_up_c, l, gates, D, bm=1024, bn=1024)
        z = matmul_ws(merged, w_out, l, bm=1024, bn=1024, out_dtype=BF16, name="out_proj")
        x2, h = postnorm(x2, z, norm_mix_post[l], norm_ffn_pre[l])
        u, wd16 = swiglu_up(h, w_ffn_gate, w_ffn_up, w_ffn_down, l, bm=2048, bn=256)
        ff = matmul_bf16_ksplit(u, wd16, bm=1024, bn=512, nk=2, out_dtype=BF16, name="ffn_down")
        gnext = norm_mix_pre[l + 1] if l + 1 < depth else None
        x2, h = postnorm(x2, ff, norm_ffn_post[l], gnext)
    return x2.reshape(B, S, D)
```

```python
import functools
import math

import jax
import jax.numpy as jnp
import numpy as np
from jax import lax
from jax.experimental import pallas as pl
from jax.experimental.pallas import tpu as pltpu

HG_HEADS = 8
HG_DK = 128
HG_DV = 128
HG_WIDTH = HG_HEADS * HG_DV
HG_CHUNK = 64
POOL_WINDOWS = (2, 4, 8, 16)
POOL_GROUPS = 4
POOL_GROUP_DIM = 256
POOL_WIDTH = POOL_GROUPS * POOL_GROUP_DIM
DA_HEADS = 8
DA_HEAD_DIM = 64
DA_VDIM = 2 * DA_HEAD_DIM
DA_WIDTH = DA_HEADS * DA_VDIM
N_BRANCH = 3
NORM_EPS = 1e-6
LOG2E = 1.4426950408889634

OFF_HQ = 0
OFF_HF = OFF_HQ + HG_HEADS * HG_DK
OFF_HV = OFF_HF + HG_HEADS * HG_DK
OFF_HG = OFF_HV + HG_WIDTH
OFF_PU = OFF_HG + HG_WIDTH
OFF_DQ = OFF_PU + POOL_WIDTH
OFF_DK = OFF_DQ + DA_WIDTH
OFF_DV = OFF_DK + DA_WIDTH
OFF_GATE = OFF_DV + DA_WIDTH

SUBLANES = 8
V7X_VMEM_LIMIT_BYTES = 56 * 1024 * 1024

BF16 = jnp.bfloat16
F32 = jnp.float32


def _cparams(*sem):
    return pltpu.CompilerParams(dimension_semantics=sem, vmem_limit_bytes=V7X_VMEM_LIMIT_BYTES)


def _sigmoid(x):
    return 0.5 * jnp.tanh(0.5 * x) + 0.5


def _rms(x, gain):
    return x * lax.rsqrt(jnp.mean(x * x, axis=-1, keepdims=True) + NORM_EPS) * gain


def _prenorm_kernel(x_ref, g_ref, h_ref):
    h_ref[...] = _rms(x_ref[...], g_ref[...]).astype(h_ref.dtype)


def prenorm(x2, gain, *, rows=256):
    T, D = x2.shape
    return pl.pallas_call(
        _prenorm_kernel,
        grid=(T // rows,),
        in_specs=[pl.BlockSpec((rows, D), lambda i: (i, 0)),
                  pl.BlockSpec((1, D), lambda i: (0, 0))],
        out_specs=pl.BlockSpec((rows, D), lambda i: (i, 0)),
        out_shape=jax.ShapeDtypeStruct((T, D), BF16),
        compiler_params=_cparams("parallel"),
        name="prenorm",
    )(x2, gain.reshape(1, D))


def _postnorm_kernel(x_ref, z_ref, gpost_ref, gnext_ref, xo_ref, h_ref):
    xn = x_ref[...] + _rms(z_ref[...].astype(F32), gpost_ref[...])
    xo_ref[...] = xn
    h_ref[...] = _rms(xn, gnext_ref[...]).astype(h_ref.dtype)


def _postnorm_last_kernel(x_ref, z_ref, gpost_ref, xo_ref):
    xo_ref[...] = x_ref[...] + _rms(z_ref[...].astype(F32), gpost_ref[...])


def postnorm(x2, z, gpost, gnext, *, rows=256):
    T, D = x2.shape
    row_spec = pl.BlockSpec((rows, D), lambda i: (i, 0))
    vec_spec = pl.BlockSpec((1, D), lambda i: (0, 0))
    if gnext is None:
        return pl.pallas_call(
            _postnorm_last_kernel,
            grid=(T // rows,),
            in_specs=[row_spec, row_spec, vec_spec],
            out_specs=row_spec,
            out_shape=jax.ShapeDtypeStruct((T, D), F32),
            compiler_params=_cparams("parallel"),
            name="postnorm_last",
        )(x2, z, gpost.reshape(1, D)), None
    return pl.pallas_call(
        _postnorm_kernel,
        grid=(T // rows,),
        in_specs=[row_spec, row_spec, vec_spec, vec_spec],
        out_specs=[row_spec, row_spec],
        out_shape=[jax.ShapeDtypeStruct((T, D), F32), jax.ShapeDtypeStruct((T, D), BF16)],
        compiler_params=_cparams("parallel"),
        name="postnorm",
    )(x2, z, gpost.reshape(1, D), gnext.reshape(1, D))


def _stream_weight_tiles(w_hbm, w16_ref, stage_ref, sem, *, layer, n_col_tiles, col0=0):
    j, i = pl.program_id(0), pl.program_id(1)
    _, k_total, bn = w16_ref.shape
    kc = stage_ref.shape[1]
    n_chunks = k_total // kc
    g = j * n_chunks + i
    n_steps = n_col_tiles * n_chunks

    def copy(tile, chunk, slot):
        return pltpu.make_async_copy(w_hbm.at[layer, pl.ds(chunk * kc, kc), pl.ds(col0 + tile * bn, bn)],
                                     stage_ref.at[slot], sem.at[slot])

    def stream_tile(step):
        return (1 + step // n_chunks) % n_col_tiles

    @pl.when(g == 0)
    def _first_tile():
        copy(0, 0, 0).start()
        for c in range(n_chunks):
            if c + 1 < n_chunks:
                copy(0, c + 1, (c + 1) % 2).start()
            copy(0, c, c % 2).wait()
            w16_ref[0, c * kc:(c + 1) * kc, :] = stage_ref[c % 2].astype(BF16)
        copy(stream_tile(0), 0, 0).start()

    @pl.when(g + 1 < n_steps)
    def _():
        copy(stream_tile(g + 1), (g + 1) % n_chunks, (g + 1) % 2).start()

    copy(stream_tile(g), i, g % 2).wait()
    w16_ref[(j + 1) % 2, pl.ds(pl.multiple_of(i * kc, kc), kc), :] = stage_ref[g % 2].astype(BF16)


def _mm_ws_kernel(a_ref, w_hbm, o_ref, w16_ref, stage_ref, sem, *, layer, n_col_tiles, col0, sigmoid_out):
    _stream_weight_tiles(w_hbm, w16_ref, stage_ref, sem, layer=layer, n_col_tiles=n_col_tiles, col0=col0)
    w16 = w16_ref[pl.program_id(0) % 2]
    acc = jnp.dot(a_ref[...], w16, preferred_element_type=F32)
    if sigmoid_out:
        acc = _sigmoid(acc)
    o_ref[...] = acc.astype(o_ref.dtype)


def _weight_stream_scratch(k_total, bn, n_row_tiles):
    assert k_total % n_row_tiles == 0 and n_row_tiles % 2 == 0
    return [pltpu.VMEM((2, k_total, bn), BF16), pltpu.VMEM((2, k_total // n_row_tiles, bn), F32),
            pltpu.SemaphoreType.DMA((2,))]


def matmul_ws(a, w_all, layer, *, bm, bn, out_dtype, name, col0=0, n_cols=None, sigmoid_out=False):
    M, K = a.shape
    N = w_all.shape[2] - col0 if n_cols is None else n_cols
    assert N % bn == 0 and M % bm == 0
    return pl.pallas_call(
        functools.partial(_mm_ws_kernel, layer=layer, n_col_tiles=N // bn, col0=col0, sigmoid_out=sigmoid_out),
        grid=(N // bn, M // bm),
        in_specs=[pl.BlockSpec((bm, K), lambda j, i: (i, 0)),
                  pl.BlockSpec(memory_space=pl.ANY)],
        out_specs=pl.BlockSpec((bm, bn), lambda j, i: (i, j)),
        out_shape=jax.ShapeDtypeStruct((M, N), out_dtype),
        scratch_shapes=_weight_stream_scratch(K, bn, M // bm),
        compiler_params=_cparams("arbitrary", "arbitrary"),
        name=name,
    )(a, w_all)


def _mm_kernel(a_ref, w_ref, o_ref):
    o_ref[...] = jnp.dot(a_ref[...], w_ref[...], preferred_element_type=F32).astype(o_ref.dtype)


def matmul_bf16(a, w, *, bm, bn, out_dtype, name):
    M, K = a.shape
    N = w.shape[1]
    return pl.pallas_call(
        _mm_kernel,
        grid=(M // bm, N // bn),
        in_specs=[pl.BlockSpec((bm, K), lambda i, j: (i, 0)),
                  pl.BlockSpec((K, bn), lambda i, j: (0, j))],
        out_specs=pl.BlockSpec((bm, bn), lambda i, j: (i, j)),
        out_shape=jax.ShapeDtypeStruct((M, N), out_dtype),
        compiler_params=_cparams("parallel", "arbitrary"),
        name=name,
    )(a, w)


def _cast_rows_stream(src_hbm, out_ref, stage_ref, sem, *, layer):
    j, i = pl.program_id(0), pl.program_id(1)
    n_row_tiles = pl.num_programs(1)
    g = j * n_row_tiles + i
    n_steps = pl.num_programs(0) * n_row_tiles
    r = stage_ref.shape[1]

    def copy(step, slot):
        return pltpu.make_async_copy(src_hbm.at[layer, pl.ds(step * r, r), :], stage_ref.at[slot], sem.at[slot])

    @pl.when(g == 0)
    def _():
        copy(0, 0).start()

    @pl.when(g + 1 < n_steps)
    def _():
        copy(g + 1, (g + 1) % 2).start()

    copy(g, g % 2).wait()
    out_ref[pl.ds(pl.multiple_of(i * r, r), r), :] = stage_ref[g % 2].astype(BF16)


def _swiglu_kernel(h_ref, wg_hbm, wu_hbm, wd_hbm, o_ref, wd16_ref,
                   wg16_ref, wg_stage, wg_sem, wu16_ref, wu_stage, wu_sem, wd_stage, wd_sem, *,
                   layer, n_col_tiles):
    _stream_weight_tiles(wg_hbm, wg16_ref, wg_stage, wg_sem, layer=layer, n_col_tiles=n_col_tiles)
    _stream_weight_tiles(wu_hbm, wu16_ref, wu_stage, wu_sem, layer=layer, n_col_tiles=n_col_tiles)
    _cast_rows_stream(wd_hbm, wd16_ref, wd_stage, wd_sem, layer=layer)
    slot = pl.program_id(0) % 2
    h = h_ref[...]
    a = jnp.dot(h, wg16_ref[slot], preferred_element_type=F32)
    b = jnp.dot(h, wu16_ref[slot], preferred_element_type=F32)
    o_ref[...] = (a * _sigmoid(a) * b).astype(o_ref.dtype)


def swiglu_up(h, wg_all, wu_all, wd_all, layer, *, bm, bn):
    M, K = h.shape
    F = wg_all.shape[2]
    D = wd_all.shape[2]
    n_row_tiles = M // bm
    assert F % bn == 0 and bn % n_row_tiles == 0
    any_spec = pl.BlockSpec(memory_space=pl.ANY)
    return pl.pallas_call(
        functools.partial(_swiglu_kernel, layer=layer, n_col_tiles=F // bn),
        grid=(F // bn, n_row_tiles),
        in_specs=[pl.BlockSpec((bm, K), lambda j, i: (i, 0)), any_spec, any_spec, any_spec],
        out_specs=[pl.BlockSpec((bm, bn), lambda j, i: (i, j)),
                   pl.BlockSpec((bn, D), lambda j, i: (j, 0))],
        out_shape=[jax.ShapeDtypeStruct((M, F), BF16), jax.ShapeDtypeStruct((F, D), BF16)],
        scratch_shapes=(_weight_stream_scratch(K, bn, n_row_tiles) + _weight_stream_scratch(K, bn, n_row_tiles)
                        + [pltpu.VMEM((2, bn // n_row_tiles, D), F32), pltpu.SemaphoreType.DMA((2,))]),
        compiler_params=_cparams("arbitrary", "arbitrary"),
        name="swiglu_up",
    )(h, wg_all, wu_all, wd_all)


def _merge_kernel(ya_ref, yb_ref, yc_ref, wa_hbm, wb_hbm, wc_hbm, ga_ref, gb_ref, gc_ref, o_ref,
                  wa16_ref, wa_stage, wa_sem, wb16_ref, wb_stage, wb_sem, wc16_ref, wc_stage, wc_sem, *,
                  layer, n_col_tiles):
    _stream_weight_tiles(wa_hbm, wa16_ref, wa_stage, wa_sem, layer=layer, n_col_tiles=n_col_tiles)
    _stream_weight_tiles(wb_hbm, wb16_ref, wb_stage, wb_sem, layer=layer, n_col_tiles=n_col_tiles)
    _stream_weight_tiles(wc_hbm, wc16_ref, wc_stage, wc_sem, layer=layer, n_col_tiles=n_col_tiles)
    slot = pl.program_id(0) % 2
    acc = ga_ref[...].astype(F32) * jnp.dot(ya_ref[...], wa16_ref[slot], preferred_element_type=F32)
    acc += gb_ref[...].astype(F32) * jnp.dot(yb_ref[...], wb16_ref[slot], preferred_element_type=F32)
    acc += gc_ref[...].astype(F32) * jnp.dot(yc_ref[...], wc16_ref[slot], preferred_element_type=F32)
    o_ref[...] = acc.astype(o_ref.dtype)


def gated_merge(ya, yb, yc, wa_all, wb_all, wc_all, layer, gates, d_model, *, bm, bn):
    M, Kb = ya.shape
    per_branch = d_model // bn
    n_row_tiles = M // bm
    y_spec = pl.BlockSpec((bm, Kb), lambda j, i: (i, 0))
    any_spec = pl.BlockSpec(memory_space=pl.ANY)

    def g_spec(k):
        return pl.BlockSpec((bm, bn), lambda j, i: (i, k * per_branch + j))

    return pl.pallas_call(
        functools.partial(_merge_kernel, layer=layer, n_col_tiles=per_branch),
        grid=(per_branch, n_row_tiles),
        in_specs=[y_spec, y_spec, y_spec, any_spec, any_spec, any_spec, g_spec(0), g_spec(1), g_spec(2)],
        out_specs=pl.BlockSpec((bm, bn), lambda j, i: (i, j)),
        out_shape=jax.ShapeDtypeStruct((M, d_model), BF16),
        scratch_shapes=_weight_stream_scratch(Kb, bn, n_row_tiles) * 3,
        compiler_params=_cparams("arbitrary", "arbitrary"),
        name="gated_merge",
    )(ya, yb, yc, wa_all, wb_all, wc_all, gates, gates, gates)


def _hgrn_tables(chunk):
    levels = int(math.log2(chunk))
    assert 1 << levels == chunk and levels >= 3
    t = np.arange(chunk)[:, None]
    u = np.arange(chunk)[None, :]
    masks = [(t == u)]
    for l in range(levels):
        m = 1 << l
        lower_t = (t % (2 * m)) >= m
        upper_u = (u % (2 * m)) < m
        masks.append(lower_t & upper_u & (t // (2 * m) == u // (2 * m)))
    return (u <= t).astype(np.float32), np.stack([m.astype(np.float32) for m in masks], axis=0)


def _split3_f32(x):
    p1 = x.astype(BF16).astype(F32)
    r1 = x - p1
    p2 = r1.astype(BF16).astype(F32)
    return p1, p2, r1 - p2


def _dot_nt(a, b):
    return lax.dot_general(a, b, (((1,), (1,)), ((), ())), preferred_element_type=F32)


def _hgrn_level_decays(f, b, chunk):
    dk = f.shape[1]
    row = lax.broadcasted_iota(jnp.int32, f.shape, 0)
    f_prev = pltpu.roll(f, 1, axis=0)
    f_next = pltpu.roll(f, chunk - 1, axis=0)
    r4 = row % 4
    decays = [jnp.where(row % 2 == 1, f, 1.0),
              jnp.where(r4 == 0, f_next, jnp.where(r4 == 1, 1.0, jnp.where(r4 == 2, f, f * f_prev)))]
    m = 4
    while m < chunk:
        ref = jnp.concatenate(
            [jnp.broadcast_to(b[p * 2 * m + m - 1:p * 2 * m + m, :], (2 * m, dk)) for p in range(chunk // (2 * m))],
            axis=0)
        diff_bits = lax.bitcast_convert_type(b - ref, jnp.uint32) | jnp.uint32(0x80000000)
        decays.append(jnp.exp(lax.bitcast_convert_type(diff_bits, F32)))
        m *= 2
    return decays


def _hgrn_kernel(q_ref, f_ref, v_ref, g_ref, lb_ref, gain_ref, tri_ref, masks_ref, o_ref, *,
                 chunk, n_chunks, unroll):
    dk = q_ref.shape[1]
    lb = lb_ref[...]
    gain = gain_ref[...]
    tri = tri_ref[...].astype(BF16)

    def body(c, st):
        rows = pl.ds(pl.multiple_of(c * chunk, chunk), chunk)
        q = q_ref[rows, :] * (HG_DK ** -0.5)
        f = lb + (1.0 - lb) * jax.nn.sigmoid(f_ref[rows, :])
        kk = 1.0 - f
        v16 = v_ref[rows, :].astype(BF16)
        parts = jnp.concatenate([p.astype(BF16) for p in _split3_f32(jnp.log(f))], axis=1)
        b3 = jnp.dot(tri, parts, preferred_element_type=F32)
        b = b3[:, :dk] + b3[:, dk:2 * dk] + b3[:, 2 * dk:]
        b_last = b[chunk - 1:chunk, :]
        q_inter = (q * jnp.exp(b)).astype(BF16)
        k_inter = (kk * jnp.exp(b_last - b)).astype(BF16)
        scores = masks_ref[0] * _dot_nt(q.astype(BF16), kk.astype(BF16)).astype(BF16)
        for l, xl in enumerate(_hgrn_level_decays(f, b, chunk)):
            scores += masks_ref[1 + l] * _dot_nt((q * xl).astype(BF16), (kk * xl).astype(BF16)).astype(BF16)
        o = jnp.dot(scores, v16, preferred_element_type=F32)
        o += _dot_nt(q_inter, st.astype(BF16))
        upd = lax.dot_general(v16, k_inter, (((0,), (0,)), ((), ())), preferred_element_type=F32)
        o = o * lax.rsqrt(jnp.mean(o * o, axis=-1, keepdims=True) + NORM_EPS) * gain
        gate = g_ref[rows, :]
        o_ref[rows, :] = (o * (gate * jax.nn.sigmoid(gate))).astype(o_ref.dtype)
        return st * jnp.exp(b_last) + upd

    lax.fori_loop(0, n_chunks, body, jnp.zeros((HG_DV, dk), F32), unroll=unroll)


def hgrn2_mixer(proj, lb, out_gain, batch, seq, *, chunk=256, unroll=2):
    T = proj.shape[0]
    tri, masks = _hgrn_tables(chunk)

    def col_spec(off):
        return pl.BlockSpec((seq, HG_DK), lambda b, h: (b, off // HG_DK + h))

    vec_spec = pl.BlockSpec((1, HG_DK), lambda b, h: (0, h))
    return pl.pallas_call(
        functools.partial(_hgrn_kernel, chunk=chunk, n_chunks=seq // chunk, unroll=unroll),
        grid=(batch, HG_HEADS),
        in_specs=[col_spec(OFF_HQ), col_spec(OFF_HF), col_spec(OFF_HV), col_spec(OFF_HG),
                  vec_spec, vec_spec,
                  pl.BlockSpec(tri.shape, lambda b, h: (0, 0)),
                  pl.BlockSpec(masks.shape, lambda b, h: (0, 0, 0))],
        out_specs=pl.BlockSpec((seq, HG_DV), lambda b, h: (b, h)),
        out_shape=jax.ShapeDtypeStruct((T, HG_WIDTH), BF16),
        compiler_params=_cparams("parallel", "parallel"),
        name="hgrn2",
    )(proj, proj, proj, proj, lb.reshape(1, -1), out_gain.reshape(1, -1),
      jnp.asarray(tri), jnp.asarray(masks, dtype=BF16))


def _pool_kernel(u_ref, w_ref, scale_ref, o_ref):
    seq = u_ref.shape[0]
    row = lax.broadcasted_iota(jnp.int32, (seq, POOL_GROUP_DIM), 0)
    for j, window in enumerate(POOL_WINDOWS):
        cols = slice(j * POOL_GROUP_DIM, (j + 1) * POOL_GROUP_DIM)
        u = u_ref[:, cols]
        acc = u
        span = 1
        while span < window:
            shifted = jnp.where(row >= span, pltpu.roll(acc, span, axis=0), 0.0)
            acc = acc + shifted
            span *= 2
        count = jnp.minimum(row + 1, window).astype(F32)
        pooled = acc / count - u
        mixed = jnp.dot(pooled.astype(BF16), w_ref[j].astype(BF16), preferred_element_type=F32)
        o_ref[:, cols] = (mixed * scale_ref[:, cols]).astype(o_ref.dtype)


def pool_mixer(proj, w_groups, scale, batch, seq):
    T = proj.shape[0]
    assert all(w & (w - 1) == 0 for w in POOL_WINDOWS)
    return pl.pallas_call(
        _pool_kernel,
        grid=(batch,),
        in_specs=[pl.BlockSpec((seq, POOL_WIDTH), lambda b: (b, OFF_PU // POOL_WIDTH)),
                  pl.BlockSpec(w_groups.shape, lambda b: (0, 0, 0)),
                  pl.BlockSpec((1, POOL_WIDTH), lambda b: (0, 0))],
        out_specs=pl.BlockSpec((seq, POOL_WIDTH), lambda b: (b, 0)),
        out_shape=jax.ShapeDtypeStruct((T, POOL_WIDTH), BF16),
        compiler_params=_cparams("parallel"),
        name="pool",
    )(proj, w_groups, scale.reshape(1, -1))


DA_AUG = 2 * DA_VDIM


def _diffattn_kernel(slopes_ref, q_ref, k_ref, v_ref, lam_ref, subln_ref, o_ref, ka_ref, va_ref, s_ref, *,
                     seq, blk, lambda_init):
    slope = slopes_ref[pl.program_id(1)]
    lp = lam_ref[...]
    lam = (jnp.exp(jnp.sum(lp[0:1] * lp[1:2], axis=-1, keepdims=True))
           - jnp.exp(jnp.sum(lp[2:3] * lp[3:4], axis=-1, keepdims=True)) + lambda_init)

    qlane = lax.broadcasted_iota(jnp.int32, (blk, DA_VDIM), 1)
    block_row = lax.broadcasted_iota(jnp.int32, (blk, DA_VDIM), 0)
    ones3 = jnp.where(qlane < 3, 1.0, 0.0).astype(BF16)
    ones_col = jnp.where(qlane == 0, 1.0, 0.0).astype(BF16)
    key_gt_query = (lax.broadcasted_iota(jnp.int32, (2 * blk, blk), 1)
                    > lax.broadcasted_iota(jnp.int32, (2 * blk, blk), 0) % blk)

    for i in range(seq // blk):
        rows = pl.ds(i * blk, blk)
        p1, p2, p3 = _split3_f32((block_row + i * blk).astype(F32) * (slope * LOG2E))
        ka_ref[rows, 0:DA_VDIM] = k_ref[rows, :].astype(BF16)
        ka_ref[rows, DA_VDIM:DA_AUG] = jnp.where(
            qlane == 0, p1, jnp.where(qlane == 1, p2, jnp.where(qlane == 2, p3, 0.0))).astype(BF16)
        va_ref[rows, 0:DA_VDIM] = v_ref[rows, :].astype(BF16)
        va_ref[rows, DA_VDIM:DA_AUG] = ones_col
        q = q_ref[rows, :] * (DA_HEAD_DIM ** -0.5 * LOG2E)
        q2 = jnp.concatenate(
            [jnp.concatenate([jnp.where(qlane < DA_HEAD_DIM, q, 0.0).astype(BF16), ones3], axis=1),
             jnp.concatenate([jnp.where(qlane >= DA_HEAD_DIM, q, 0.0).astype(BF16), ones3], axis=1)],
            axis=0)
        m_tile = jnp.full((2 * blk, DA_VDIM), -jnp.inf, F32)
        for j in range(i + 1):
            keys = pl.ds(j * blk, blk)
            s = _dot_nt(q2, ka_ref[keys, :])
            if j == i:
                s = jnp.where(key_gt_query, -jnp.inf, s)
            s_ref[:, keys] = s
            for c in range(blk // DA_VDIM):
                m_tile = jnp.maximum(m_tile, s[:, c * DA_VDIM:(c + 1) * DA_VDIM])
        m_row = jnp.max(m_tile, axis=-1, keepdims=True)
        acc = jnp.zeros((2 * blk, DA_AUG), F32)
        for j in range(i + 1):
            keys = pl.ds(j * blk, blk)
            p = jnp.exp2(s_ref[:, keys] - m_row).astype(BF16)
            acc += jnp.dot(p, va_ref[keys, :], preferred_element_type=F32)
        out0 = acc[0:blk, 0:DA_VDIM] * (1.0 / acc[0:blk, DA_VDIM:DA_VDIM + 1])
        out1 = acc[blk:2 * blk, 0:DA_VDIM] * (1.0 / acc[blk:2 * blk, DA_VDIM:DA_VDIM + 1])
        o = out0 - lam * out1
        o = o * lax.rsqrt(jnp.mean(o * o, axis=-1, keepdims=True) + NORM_EPS) * subln_ref[...]
        o_ref[rows, :] = (o * (1.0 - lambda_init)).astype(o_ref.dtype)


def diff_attention(proj, lam_params, subln, lambda_init, batch, seq, *, blk=256):
    T = proj.shape[0]
    slopes = (2.0 ** (-8.0 * jnp.arange(1, DA_HEADS + 1, dtype=F32) / DA_HEADS)).astype(F32)

    def col_spec(off):
        return pl.BlockSpec((seq, DA_VDIM), lambda b, h, s: (b, off // DA_VDIM + h))

    grid_spec = pltpu.PrefetchScalarGridSpec(
        num_scalar_prefetch=1,
        grid=(batch, DA_HEADS),
        in_specs=[col_spec(OFF_DQ), col_spec(OFF_DK), col_spec(OFF_DV),
                  pl.BlockSpec(lam_params.shape, lambda b, h, s: (0, 0)),
                  pl.BlockSpec((1, DA_VDIM), lambda b, h, s: (0, 0))],
        out_specs=pl.BlockSpec((seq, DA_VDIM), lambda b, h, s: (b, h)),
        scratch_shapes=[pltpu.VMEM((seq, DA_AUG), BF16), pltpu.VMEM((seq, DA_AUG), BF16),
                        pltpu.VMEM((2 * blk, seq), F32)],
    )
    return pl.pallas_call(
        functools.partial(_diffattn_kernel, seq=seq, blk=blk, lambda_init=lambda_init),
        grid_spec=grid_spec,
        out_shape=jax.ShapeDtypeStruct((T, DA_WIDTH), BF16),
        compiler_params=_cparams("parallel", "parallel"),
        name="diffattn",
    )(slopes, proj, proj, proj, lam_params, subln.reshape(1, -1))


def kernel(x, norm_mix_pre, norm_mix_post, norm_ffn_pre, norm_ffn_post, w_in, hgrn_lb_logits, hgrn_out_norm,
           pool_w, pool_scale, diff_lambda, diff_subln, w_up_a, w_up_b, w_up_c, w_out, w_ffn_gate, w_ffn_up,
           w_ffn_down):
    B, S, D = x.shape
    depth = w_in.shape[0]
    T = B * S
    lb_all = jnp.cumsum(jax.nn.softmax(hgrn_lb_logits.astype(F32), axis=0), axis=0)
    lb_all = lb_all - lb_all[0:1]

    x2 = x.reshape(T, D)
    h = prenorm(x2, norm_mix_pre[0])
    for l in range(depth):
        lambda_init = 0.8 - 0.6 * math.exp(-0.3 * l)
        proj = matmul_ws(h, w_in, l, bm=1024, bn=1024, out_dtype=F32, name="in_proj_mix", n_cols=OFF_GATE)
        gates = matmul_ws(h, w_in, l, bm=1024, bn=1024, out_dtype=BF16, name="in_proj_gate", col0=OFF_GATE,
                          sigmoid_out=True)
        y_a = hgrn2_mixer(proj, lb_all[l], hgrn_out_norm[l], B, S)
        y_b = pool_mixer(proj, pool_w[l], pool_scale[l], B, S)
        y_c = diff_attention(proj, diff_lambda[l], diff_subln[l], lambda_init, B, S)
        merged = gated_merge(y_a, y_b, y_c, w_up_a, w_up_b, w_up_c, l, gates, D, bm=1024, bn=1024)
        z = matmul_ws(merged, w_out, l, bm=1024, bn=1024, out_dtype=BF16, name="out_proj")
        x2, h = postnorm(x2, z, norm_mix_post[l], norm_ffn_pre[l])
        u, wd16 = swiglu_up(h, w_ffn_gate, w_ffn_up, w_ffn_down, l, bm=2048, bn=256)
        ff = matmul_bf16(u, wd16, bm=512, bn=512, out_dtype=BF16, name="ffn_down")
        gnext = norm_mix_pre[l + 1] if l + 1 < depth else None
        x2, h = postnorm(x2, ff, norm_ffn_post[l], gnext)
    return x2.reshape(B, S, D)
```

```python
import functools
import math

import jax
import jax.numpy as jnp
import numpy as np
from jax import lax
from jax.experimental import pallas as pl
from jax.experimental.pallas import tpu as pltpu

HG_HEADS = 8
HG_DK = 128
HG_DV = 128
HG_WIDTH = HG_HEADS * HG_DV
HG_CHUNK = 64
POOL_WINDOWS = (2, 4, 8, 16)
POOL_GROUPS = 4
POOL_GROUP_DIM = 256
POOL_WIDTH = POOL_GROUPS * POOL_GROUP_DIM
DA_HEADS = 8
DA_HEAD_DIM = 64
DA_VDIM = 2 * DA_HEAD_DIM
DA_WIDTH = DA_HEADS * DA_VDIM
N_BRANCH = 3
NORM_EPS = 1e-6
LOG2E = 1.4426950408889634

OFF_HQ = 0
OFF_HF = OFF_HQ + HG_HEADS * HG_DK
OFF_HV = OFF_HF + HG_HEADS * HG_DK
OFF_HG = OFF_HV + HG_WIDTH
OFF_PU = OFF_HG + HG_WIDTH
OFF_DQ = OFF_PU + POOL_WIDTH
OFF_DK = OFF_DQ + DA_WIDTH
OFF_DV = OFF_DK + DA_WIDTH
OFF_GATE = OFF_DV + DA_WIDTH

SUBLANES = 8
V7X_VMEM_LIMIT_BYTES = 56 * 1024 * 1024

BF16 = jnp.bfloat16
F32 = jnp.float32


def _cparams(*sem):
    return pltpu.CompilerParams(dimension_semantics=sem, vmem_limit_bytes=V7X_VMEM_LIMIT_BYTES)


def _sigmoid(x):
    return 0.5 * jnp.tanh(0.5 * x) + 0.5


def _rms(x, gain):
    return x * lax.rsqrt(jnp.mean(x * x, axis=-1, keepdims=True) + NORM_EPS) * gain


def _prenorm_kernel(x_ref, g_ref, h_ref):
    h_ref[...] = _rms(x_ref[...], g_ref[...]).astype(h_ref.dtype)


def prenorm(x2, gain, *, rows=512):
    T, D = x2.shape
    return pl.pallas_call(
        _prenorm_kernel,
        grid=(T // rows,),
        in_specs=[pl.BlockSpec((rows, D), lambda i: (i, 0)),
                  pl.BlockSpec((1, D), lambda i: (0, 0))],
        out_specs=pl.BlockSpec((rows, D), lambda i: (i, 0)),
        out_shape=jax.ShapeDtypeStruct((T, D), BF16),
        compiler_params=_cparams("parallel"),
        name="prenorm",
    )(x2, gain.reshape(1, D))


def _postnorm_kernel(x_ref, z_ref, gpost_ref, gnext_ref, xo_ref, h_ref):
    xn = x_ref[...] + _rms(z_ref[...].astype(F32), gpost_ref[...])
    xo_ref[...] = xn
    h_ref[...] = _rms(xn, gnext_ref[...]).astype(h_ref.dtype)


def _postnorm_last_kernel(x_ref, z_ref, gpost_ref, xo_ref):
    xo_ref[...] = x_ref[...] + _rms(z_ref[...].astype(F32), gpost_ref[...])


def postnorm(x2, z, gpost, gnext, *, rows=256):
    T, D = x2.shape
    row_spec = pl.BlockSpec((rows, D), lambda i: (i, 0))
    vec_spec = pl.BlockSpec((1, D), lambda i: (0, 0))
    if gnext is None:
        return pl.pallas_call(
            _postnorm_last_kernel,
            grid=(T // rows,),
            in_specs=[row_spec, row_spec, vec_spec],
            out_specs=row_spec,
            out_shape=jax.ShapeDtypeStruct((T, D), F32),
            compiler_params=_cparams("parallel"),
            name="postnorm_last",
        )(x2, z, gpost.reshape(1, D)), None
    return pl.pallas_call(
        _postnorm_kernel,
        grid=(T // rows,),
        in_specs=[row_spec, row_spec, vec_spec, vec_spec],
        out_specs=[row_spec, row_spec],
        out_shape=[jax.ShapeDtypeStruct((T, D), F32), jax.ShapeDtypeStruct((T, D), BF16)],
        compiler_params=_cparams("parallel"),
        name="postnorm",
    )(x2, z, gpost.reshape(1, D), gnext.reshape(1, D))


def _stream_weight_tiles(w_hbm, w16_ref, stage_ref, sem, *, layer, n_col_tiles, col0=0):
    j, i = pl.program_id(0), pl.program_id(1)
    _, k_total, bn = w16_ref.shape
    kc = stage_ref.shape[1]
    n_chunks = k_total // kc
    g = j * n_chunks + i
    n_steps = n_col_tiles * n_chunks

    def copy(tile, chunk, slot):
        return pltpu.make_async_copy(w_hbm.at[layer, pl.ds(chunk * kc, kc), pl.ds(col0 + tile * bn, bn)],
                                     stage_ref.at[slot], sem.at[slot])

    def stream_tile(step):
        return (1 + step // n_chunks) % n_col_tiles

    @pl.when(g == 0)
    def _first_tile():
        copy(0, 0, 0).start()
        for c in range(n_chunks):
            if c + 1 < n_chunks:
                copy(0, c + 1, (c + 1) % 2).start()
            copy(0, c, c % 2).wait()
            w16_ref[0, c * kc:(c + 1) * kc, :] = stage_ref[c % 2].astype(BF16)
        copy(stream_tile(0), 0, 0).start()

    @pl.when(g + 1 < n_steps)
    def _():
        copy(stream_tile(g + 1), (g + 1) % n_chunks, (g + 1) % 2).start()

    copy(stream_tile(g), i, g % 2).wait()
    w16_ref[(j + 1) % 2, pl.ds(pl.multiple_of(i * kc, kc), kc), :] = stage_ref[g % 2].astype(BF16)


def _mm_ws_kernel(a_ref, w_hbm, o_ref, w16_ref, stage_ref, sem, *, layer, n_col_tiles, col0, sigmoid_out):
    _stream_weight_tiles(w_hbm, w16_ref, stage_ref, sem, layer=layer, n_col_tiles=n_col_tiles, col0=col0)
    w16 = w16_ref[pl.program_id(0) % 2]
    acc = jnp.dot(a_ref[...], w16, preferred_element_type=F32)
    if sigmoid_out:
        acc = _sigmoid(acc)
    o_ref[...] = acc.astype(o_ref.dtype)


def _weight_stream_scratch(k_total, bn, n_row_tiles):
    assert k_total % n_row_tiles == 0 and n_row_tiles % 2 == 0
    return [pltpu.VMEM((2, k_total, bn), BF16), pltpu.VMEM((2, k_total // n_row_tiles, bn), F32),
            pltpu.SemaphoreType.DMA((2,))]


def matmul_ws(a, w_all, layer, *, bm, bn, out_dtype, name, col0=0, n_cols=None, sigmoid_out=False):
    M, K = a.shape
    N = w_all.shape[2] - col0 if n_cols is None else n_cols
    assert N % bn == 0 and M % bm == 0
    return pl.pallas_call(
        functools.partial(_mm_ws_kernel, layer=layer, n_col_tiles=N // bn, col0=col0, sigmoid_out=sigmoid_out),
        grid=(N // bn, M // bm),
        in_specs=[pl.BlockSpec((bm, K), lambda j, i: (i, 0)),
                  pl.BlockSpec(memory_space=pl.ANY)],
        out_specs=pl.BlockSpec((bm, bn), lambda j, i: (i, j)),
        out_shape=jax.ShapeDtypeStruct((M, N), out_dtype),
        scratch_shapes=_weight_stream_scratch(K, bn, M // bm),
        compiler_params=_cparams("arbitrary", "arbitrary"),
        name=name,
    )(a, w_all)


def _mm_kernel(a_ref, w_ref, o_ref):
    o_ref[...] = jnp.dot(a_ref[...], w_ref[...], preferred_element_type=F32).astype(o_ref.dtype)


def matmul_bf16(a, w, *, bm, bn, out_dtype, name):
    M, K = a.shape
    N = w.shape[1]
    return pl.pallas_call(
        _mm_kernel,
        grid=(M // bm, N // bn),
        in_specs=[pl.BlockSpec((bm, K), lambda i, j: (i, 0)),
                  pl.BlockSpec((K, bn), lambda i, j: (0, j))],
        out_specs=pl.BlockSpec((bm, bn), lambda i, j: (i, j)),
        out_shape=jax.ShapeDtypeStruct((M, N), out_dtype),
        compiler_params=_cparams("parallel", "arbitrary"),
        name=name,
    )(a, w)


def _cast_rows_stream(src_hbm, out_ref, stage_ref, sem, *, layer):
    j, i = pl.program_id(0), pl.program_id(1)
    n_row_tiles = pl.num_programs(1)
    g = j * n_row_tiles + i
    n_steps = pl.num_programs(0) * n_row_tiles
    r = stage_ref.shape[1]

    def copy(step, slot):
        return pltpu.make_async_copy(src_hbm.at[layer, pl.ds(step * r, r), :], stage_ref.at[slot], sem.at[slot])

    @pl.when(g == 0)
    def _():
        copy(0, 0).start()

    @pl.when(g + 1 < n_steps)
    def _():
        copy(g + 1, (g + 1) % 2).start()

    copy(g, g % 2).wait()
    out_ref[pl.ds(pl.multiple_of(i * r, r), r), :] = stage_ref[g % 2].astype(BF16)


def _swiglu_kernel(h_ref, wg_hbm, wu_hbm, wd_hbm, o_ref, wd16_ref,
                   wg16_ref, wg_stage, wg_sem, wu16_ref, wu_stage, wu_sem, wd_stage, wd_sem, *,
                   layer, n_col_tiles):
    _stream_weight_tiles(wg_hbm, wg16_ref, wg_stage, wg_sem, layer=layer, n_col_tiles=n_col_tiles)
    _stream_weight_tiles(wu_hbm, wu16_ref, wu_stage, wu_sem, layer=layer, n_col_tiles=n_col_tiles)
    _cast_rows_stream(wd_hbm, wd16_ref, wd_stage, wd_sem, layer=layer)
    slot = pl.program_id(0) % 2
    h = h_ref[...]
    a = jnp.dot(h, wg16_ref[slot], preferred_element_type=F32)
    b = jnp.dot(h, wu16_ref[slot], preferred_element_type=F32)
    o_ref[...] = (a * _sigmoid(a) * b).astype(o_ref.dtype)


def swiglu_up(h, wg_all, wu_all, wd_all, layer, *, bm, bn):
    M, K = h.shape
    F = wg_all.shape[2]
    D = wd_all.shape[2]
    n_row_tiles = M // bm
    assert F % bn == 0 and bn % n_row_tiles == 0
    any_spec = pl.BlockSpec(memory_space=pl.ANY)
    return pl.pallas_call(
        functools.partial(_swiglu_kernel, layer=layer, n_col_tiles=F // bn),
        grid=(F // bn, n_row_tiles),
        in_specs=[pl.BlockSpec((bm, K), lambda j, i: (i, 0)), any_spec, any_spec, any_spec],
        out_specs=[pl.BlockSpec((bm, bn), lambda j, i: (i, j)),
                   pl.BlockSpec((bn, D), lambda j, i: (j, 0))],
        out_shape=[jax.ShapeDtypeStruct((M, F), BF16), jax.ShapeDtypeStruct((F, D), BF16)],
        scratch_shapes=(_weight_stream_scratch(K, bn, n_row_tiles) + _weight_stream_scratch(K, bn, n_row_tiles)
                        + [pltpu.VMEM((2, bn // n_row_tiles, D), F32), pltpu.SemaphoreType.DMA((2,))]),
        compiler_params=_cparams("arbitrary", "arbitrary"),
        name="swiglu_up",
    )(h, wg_all, wu_all, wd_all)


def _merge_kernel(ya_ref, yb_ref, yc_ref, wa_hbm, wb_hbm, wc_hbm, ga_ref, gb_ref, gc_ref, o_ref,
                  wa16_ref, wa_stage, wa_sem, wb16_ref, wb_stage, wb_sem, wc16_ref, wc_stage, wc_sem, *,
                  layer, n_col_tiles):
    _stream_weight_tiles(wa_hbm, wa16_ref, wa_stage, wa_sem, layer=layer, n_col_tiles=n_col_tiles)
    _stream_weight_tiles(wb_hbm, wb16_ref, wb_stage, wb_sem, layer=layer, n_col_tiles=n_col_tiles)
    _stream_weight_tiles(wc_hbm, wc16_ref, wc_stage, wc_sem, layer=layer, n_col_tiles=n_col_tiles)
    slot = pl.program_id(0) % 2
    acc = ga_ref[...].astype(F32) * jnp.dot(ya_ref[...], wa16_ref[slot], preferred_element_type=F32)
    acc += gb_ref[...].astype(F32) * jnp.dot(yb_ref[...], wb16_ref[slot], preferred_element_type=F32)
    acc += gc_ref[...].astype(F32) * jnp.dot(yc_ref[...], wc16_ref[slot], preferred_element_type=F32)
    o_ref[...] = acc.astype(o_ref.dtype)


def gated_merge(ya, yb, yc, wa_all, wb_all, wc_all, layer, gates, d_model, *, bm, bn):
    M, Kb = ya.shape
    per_branch = d_model // bn
    n_row_tiles = M // bm
    y_spec = pl.BlockSpec((bm, Kb), lambda j, i: (i, 0))
    any_spec = pl.BlockSpec(memory_space=pl.ANY)

    def g_spec(k):
        return pl.BlockSpec((bm, bn), lambda j, i: (i, k * per_branch + j))

    return pl.pallas_call(
        functools.partial(_merge_kernel, layer=layer, n_col_tiles=per_branch),
        grid=(per_branch, n_row_tiles),
        in_specs=[y_spec, y_spec, y_spec, any_spec, any_spec, any_spec, g_spec(0), g_spec(1), g_spec(2)],
        out_specs=pl.BlockSpec((bm, bn), lambda j, i: (i, j)),
        out_shape=jax.ShapeDtypeStruct((M, d_model), BF16),
        scratch_shapes=_weight_stream_scratch(Kb, bn, n_row_tiles) * 3,
        compiler_params=_cparams("arbitrary", "arbitrary"),
        name="gated_merge",
    )(ya, yb, yc, wa_all, wb_all, wc_all, gates, gates, gates)


def _hgrn_tables(chunk):
    levels = int(math.log2(chunk))
    assert 1 << levels == chunk and levels >= 3
    t = np.arange(chunk)[:, None]
    u = np.arange(chunk)[None, :]
    masks = [(t == u)]
    for l in range(levels):
        m = 1 << l
        lower_t = (t % (2 * m)) >= m
        upper_u = (u % (2 * m)) < m
        masks.append(lower_t & upper_u & (t // (2 * m) == u // (2 * m)))
    return (u <= t).astype(np.float32), np.stack([m.astype(np.float32) for m in masks], axis=0)


def _split3_f32(x):
    p1 = x.astype(BF16).astype(F32)
    r1 = x - p1
    p2 = r1.astype(BF16).astype(F32)
    return p1, p2, r1 - p2


def _dot_nt(a, b):
    return lax.dot_general(a, b, (((1,), (1,)), ((), ())), preferred_element_type=F32)


def _hgrn_level_decays(f, b, chunk):
    dk = f.shape[1]
    row = lax.broadcasted_iota(jnp.int32, f.shape, 0)
    f_prev = pltpu.roll(f, 1, axis=0)
    f_next = pltpu.roll(f, chunk - 1, axis=0)
    r4 = row % 4
    decays = [jnp.where(row % 2 == 1, f, 1.0),
              jnp.where(r4 == 0, f_next, jnp.where(r4 == 1, 1.0, jnp.where(r4 == 2, f, f * f_prev)))]
    m = 4
    while m < chunk:
        ref = jnp.concatenate(
            [jnp.broadcast_to(b[p * 2 * m + m - 1:p * 2 * m + m, :], (2 * m, dk)) for p in range(chunk // (2 * m))],
            axis=0)
        lower = (row % (2 * m)) >= m
        decays.append(jnp.exp(jnp.where(lower, b - ref, ref - b)))
        m *= 2
    return decays


def _hgrn_kernel(q_ref, f_ref, v_ref, g_ref, lb_ref, gain_ref, tri_ref, masks_ref, o_ref, *,
                 chunk, n_chunks, unroll):
    dk = q_ref.shape[1]
    lb = lb_ref[...]
    gain = gain_ref[...]
    tri = tri_ref[...].astype(BF16)

    def body(c, st):
        rows = pl.ds(pl.multiple_of(c * chunk, chunk), chunk)
        q = q_ref[rows, :] * (HG_DK ** -0.5)
        f = lb + (1.0 - lb) * jax.nn.sigmoid(f_ref[rows, :])
        kk = 1.0 - f
        v16 = v_ref[rows, :].astype(BF16)
        parts = jnp.concatenate([p.astype(BF16) for p in _split3_f32(jnp.log(f))], axis=1)
        b3 = jnp.dot(tri, parts, preferred_element_type=F32)
        b = b3[:, :dk] + b3[:, dk:2 * dk] + b3[:, 2 * dk:]
        b_last = b[chunk - 1:chunk, :]
        q_inter = (q * jnp.exp(b)).astype(BF16)
        k_inter = (kk * jnp.exp(b_last - b)).astype(BF16)
        scores = masks_ref[0] * _dot_nt(q.astype(BF16), kk.astype(BF16)).astype(BF16)
        for l, xl in enumerate(_hgrn_level_decays(f, b, chunk)):
            scores += masks_ref[1 + l] * _dot_nt((q * xl).astype(BF16), (kk * xl).astype(BF16)).astype(BF16)
        o = jnp.dot(scores, v16, preferred_element_type=F32)
        o += _dot_nt(q_inter, st.astype(BF16))
        upd = lax.dot_general(v16, k_inter, (((0,), (0,)), ((), ())), preferred_element_type=F32)
        o = o * lax.rsqrt(jnp.mean(o * o, axis=-1, keepdims=True) + NORM_EPS) * gain
        gate = g_ref[rows, :]
        o_ref[rows, :] = (o * (gate * jax.nn.sigmoid(gate))).astype(o_ref.dtype)
        return st * jnp.exp(b_last) + upd

    lax.fori_loop(0, n_chunks, body, jnp.zeros((HG_DV, dk), F32), unroll=unroll)


def hgrn2_mixer(proj, lb, out_gain, batch, seq, *, chunk=256, unroll=2):
    T = proj.shape[0]
    tri, masks = _hgrn_tables(chunk)

    def col_spec(off):
        return pl.BlockSpec((seq, HG_DK), lambda b, h: (b, off // HG_DK + h))

    vec_spec = pl.BlockSpec((1, HG_DK), lambda b, h: (0, h))
    return pl.pallas_call(
        functools.partial(_hgrn_kernel, chunk=chunk, n_chunks=seq // chunk, unroll=unroll),
        grid=(batch, HG_HEADS),
        in_specs=[col_spec(OFF_HQ), col_spec(OFF_HF), col_spec(OFF_HV), col_spec(OFF_HG),
                  vec_spec, vec_spec,
                  pl.BlockSpec(tri.shape, lambda b, h: (0, 0)),
                  pl.BlockSpec(masks.shape, lambda b, h: (0, 0, 0))],
        out_specs=pl.BlockSpec((seq, HG_DV), lambda b, h: (b, h)),
        out_shape=jax.ShapeDtypeStruct((T, HG_WIDTH), BF16),
        compiler_params=_cparams("parallel", "parallel"),
        name="hgrn2",
    )(proj, proj, proj, proj, lb.reshape(1, -1), out_gain.reshape(1, -1),
      jnp.asarray(tri), jnp.asarray(masks, dtype=BF16))


def _pool_kernel(u_ref, w_ref, scale_ref, o_ref):
    seq = u_ref.shape[0]
    row = lax.broadcasted_iota(jnp.int32, (seq, POOL_GROUP_DIM), 0)
    for j, window in enumerate(POOL_WINDOWS):
        cols = slice(j * POOL_GROUP_DIM, (j + 1) * POOL_GROUP_DIM)
        u = u_ref[:, cols]
        acc = u
        span = 1
        while span < window:
            shifted = jnp.where(row >= span, pltpu.roll(acc, span, axis=0), 0.0)
            acc = acc + shifted
            span *= 2
        count = jnp.minimum(row + 1, window).astype(F32)
        pooled = acc / count - u
        mixed = jnp.dot(pooled.astype(BF16), w_ref[j].astype(BF16), preferred_element_type=F32)
        o_ref[:, cols] = (mixed * scale_ref[:, cols]).astype(o_ref.dtype)


def pool_mixer(proj, w_groups, scale, batch, seq):
    T = proj.shape[0]
    assert all(w & (w - 1) == 0 for w in POOL_WINDOWS)
    return pl.pallas_call(
        _pool_kernel,
        grid=(batch,),
        in_specs=[pl.BlockSpec((seq, POOL_WIDTH), lambda b: (b, OFF_PU // POOL_WIDTH)),
                  pl.BlockSpec(w_groups.shape, lambda b: (0, 0, 0)),
                  pl.BlockSpec((1, POOL_WIDTH), lambda b: (0, 0))],
        out_specs=pl.BlockSpec((seq, POOL_WIDTH), lambda b: (b, 0)),
        out_shape=jax.ShapeDtypeStruct((T, POOL_WIDTH), BF16),
        compiler_params=_cparams("parallel"),
        name="pool",
    )(proj, w_groups, scale.reshape(1, -1))


DA_AUG = 2 * DA_VDIM


def _diffattn_kernel(slopes_ref, q_ref, k_ref, v_ref, lam_ref, subln_ref, o_ref, ka_ref, va_ref, s_ref, *,
                     seq, blk, lambda_init):
    slope = slopes_ref[pl.program_id(1)]
    lp = lam_ref[...]
    lam = (jnp.exp(jnp.sum(lp[0:1] * lp[1:2], axis=-1, keepdims=True))
           - jnp.exp(jnp.sum(lp[2:3] * lp[3:4], axis=-1, keepdims=True)) + lambda_init)

    qlane = lax.broadcasted_iota(jnp.int32, (blk, DA_VDIM), 1)
    block_row = lax.broadcasted_iota(jnp.int32, (blk, DA_VDIM), 0)
    ones3 = jnp.where(qlane < 3, 1.0, 0.0).astype(BF16)
    ones_col = jnp.where(qlane == 0, 1.0, 0.0).astype(BF16)
    key_gt_query = (lax.broadcasted_iota(jnp.int32, (2 * blk, blk), 1)
                    > lax.broadcasted_iota(jnp.int32, (2 * blk, blk), 0) % blk)

    for i in range(seq // blk):
        rows = pl.ds(i * blk, blk)
        p1, p2, p3 = _split3_f32((block_row + i * blk).astype(F32) * (slope * LOG2E))
        ka_ref[rows, 0:DA_VDIM] = k_ref[rows, :].astype(BF16)
        ka_ref[rows, DA_VDIM:DA_AUG] = jnp.where(
            qlane == 0, p1, jnp.where(qlane == 1, p2, jnp.where(qlane == 2, p3, 0.0))).astype(BF16)
        va_ref[rows, 0:DA_VDIM] = v_ref[rows, :].astype(BF16)
        va_ref[rows, DA_VDIM:DA_AUG] = ones_col
        q = q_ref[rows, :] * (DA_HEAD_DIM ** -0.5 * LOG2E)
        q2 = jnp.concatenate(
            [jnp.concatenate([jnp.where(qlane < DA_HEAD_DIM, q, 0.0).astype(BF16), ones3], axis=1),
             jnp.concatenate([jnp.where(qlane >= DA_HEAD_DIM, q, 0.0).astype(BF16), ones3], axis=1)],
            axis=0)
        m_tile = jnp.full((2 * blk, DA_VDIM), -jnp.inf, F32)
        for j in range(i + 1):
            keys = pl.ds(j * blk, blk)
            s = _dot_nt(q2, ka_ref[keys, :])
            if j == i:
                s = jnp.where(key_gt_query, -jnp.inf, s)
            s_ref[:, keys] = s
            for c in range(blk // DA_VDIM):
                m_tile = jnp.maximum(m_tile, s[:, c * DA_VDIM:(c + 1) * DA_VDIM])
        m_row = jnp.max(m_tile, axis=-1, keepdims=True)
        acc = jnp.zeros((2 * blk, DA_AUG), F32)
        for j in range(i + 1):
            keys = pl.ds(j * blk, blk)
            p = jnp.exp2(s_ref[:, keys] - m_row).astype(BF16)
            acc += jnp.dot(p, va_ref[keys, :], preferred_element_type=F32)
        out0 = acc[0:blk, 0:DA_VDIM] * (1.0 / acc[0:blk, DA_VDIM:DA_VDIM + 1])
        out1 = acc[blk:2 * blk, 0:DA_VDIM] * (1.0 / acc[blk:2 * blk, DA_VDIM:DA_VDIM + 1])
        o = out0 - lam * out1
        o = o * lax.rsqrt(jnp.mean(o * o, axis=-1, keepdims=True) + NORM_EPS) * subln_ref[...]
        o_ref[rows, :] = (o * (1.0 - lambda_init)).astype(o_ref.dtype)


def diff_attention(proj, lam_params, subln, lambda_init, batch, seq, *, blk=256):
    T = proj.shape[0]
    slopes = (2.0 ** (-8.0 * jnp.arange(1, DA_HEADS + 1, dtype=F32) / DA_HEADS)).astype(F32)

    def col_spec(off):
        return pl.BlockSpec((seq, DA_VDIM), lambda b, h, s: (b, off // DA_VDIM + h))

    grid_spec = pltpu.PrefetchScalarGridSpec(
        num_scalar_prefetch=1,
        grid=(batch, DA_HEADS),
        in_specs=[col_spec(OFF_DQ), col_spec(OFF_DK), col_spec(OFF_DV),
                  pl.BlockSpec(lam_params.shape, lambda b, h, s: (0, 0)),
                  pl.BlockSpec((1, DA_VDIM), lambda b, h, s: (0, 0))],
        out_specs=pl.BlockSpec((seq, DA_VDIM), lambda b, h, s: (b, h)),
        scratch_shapes=[pltpu.VMEM((seq, DA_AUG), BF16), pltpu.VMEM((seq, DA_AUG), BF16),
                        pltpu.VMEM((2 * blk, seq), F32)],
    )
    return pl.pallas_call(
        functools.partial(_diffattn_kernel, seq=seq, blk=blk, lambda_init=lambda_init),
        grid_spec=grid_spec,
        out_shape=jax.ShapeDtypeStruct((T, DA_WIDTH), BF16),
        compiler_params=_cparams("parallel", "parallel"),
        name="diffattn",
    )(slopes, proj, proj, proj, lam_params, subln.reshape(1, -1))


def kernel(x, norm_mix_pre, norm_mix_post, norm_ffn_pre, norm_ffn_post, w_in, hgrn_lb_logits, hgrn_out_norm,
           pool_w, pool_scale, diff_lambda, diff_subln, w_up_a, w_up_b, w_up_c, w_out, w_ffn_gate, w_ffn_up,
           w_ffn_down):
    B, S, D = x.shape
    depth = w_in.shape[0]
    T = B * S
    lb_all = jnp.cumsum(jax.nn.softmax(hgrn_lb_logits.astype(F32), axis=0), axis=0)
    lb_all = lb_all - lb_all[0:1]

    x2 = x.reshape(T, D)
    h = prenorm(x2, norm_mix_pre[0])
    for l in range(depth):
        lambda_init = 0.8 - 0.6 * math.exp(-0.3 * l)
        proj = matmul_ws(h, w_in, l, bm=1024, bn=1024, out_dtype=F32, name="in_proj_mix", n_cols=OFF_GATE)
        gates = matmul_ws(h, w_in, l, bm=1024, bn=1024, out_dtype=BF16, name="in_proj_gate", col0=OFF_GATE,
                          sigmoid_out=True)
        y_a = hgrn2_mixer(proj, lb_all[l], hgrn_out_norm[l], B, S)
        y_b = pool_mixer(proj, pool_w[l], pool_scale[l], B, S)
        y_c = diff_attention(proj, diff_lambda[l], diff_subln[l], lambda_init, B, S)
        merged = gated_merge(y_a, y_b, y_c, w_up_a, w_up_b, w_up_c, l, gates, D, bm=1024, bn=1024)
        z = matmul_ws(merged, w_out, l, bm=1024, bn=1024, out_dtype=BF16, name="out_proj")
        x2, h = postnorm(x2, z, norm_mix_post[l], norm_ffn_pre[l])
        u, wd16 = swiglu_up(h, w_ffn_gate, w_ffn_up, w_ffn_down, l, bm=2048, bn=256)
        ff = matmul_bf16(u, wd16, bm=512, bn=512, out_dtype=BF16, name="ffn_down")
        gnext = norm_mix_pre[l + 1] if l + 1 < depth else None
        x2, h = postnorm(x2, ff, norm_ffn_post[l], gnext)
    return x2.reshape(B, S, D)
```

```python
import functools
import math

import jax
import jax.numpy as jnp
import numpy as np
from jax import lax
from jax.experimental import pallas as pl
from jax.experimental.pallas import tpu as pltpu

HG_HEADS = 8
HG_DK = 128
HG_DV = 128
HG_WIDTH = HG_HEADS * HG_DV
HG_CHUNK = 64
POOL_WINDOWS = (2, 4, 8, 16)
POOL_GROUPS = 4
POOL_GROUP_DIM = 256
POOL_WIDTH = POOL_GROUPS * POOL_GROUP_DIM
DA_HEADS = 8
DA_HEAD_DIM = 64
DA_VDIM = 2 * DA_HEAD_DIM
DA_WIDTH = DA_HEADS * DA_VDIM
N_BRANCH = 3
NORM_EPS = 1e-6
LOG2E = 1.4426950408889634

OFF_HQ = 0
OFF_HF = OFF_HQ + HG_HEADS * HG_DK
OFF_HV = OFF_HF + HG_HEADS * HG_DK
OFF_HG = OFF_HV + HG_WIDTH
OFF_PU = OFF_HG + HG_WIDTH
OFF_DQ = OFF_PU + POOL_WIDTH
OFF_DK = OFF_DQ + DA_WIDTH
OFF_DV = OFF_DK + DA_WIDTH
OFF_GATE = OFF_DV + DA_WIDTH

SUBLANES = 8
V7X_VMEM_LIMIT_BYTES = 56 * 1024 * 1024

BF16 = jnp.bfloat16
F32 = jnp.float32


def _cparams(*sem):
    return pltpu.CompilerParams(dimension_semantics=sem, vmem_limit_bytes=V7X_VMEM_LIMIT_BYTES)


def _sigmoid(x):
    return 0.5 * jnp.tanh(0.5 * x) + 0.5


def _rms(x, gain):
    return x * lax.rsqrt(jnp.mean(x * x, axis=-1, keepdims=True) + NORM_EPS) * gain


def _prenorm_kernel(x_ref, g_ref, h_ref):
    h_ref[...] = _rms(x_ref[...], g_ref[...]).astype(h_ref.dtype)


def prenorm(x2, gain, *, rows=512):
    T, D = x2.shape
    return pl.pallas_call(
        _prenorm_kernel,
        grid=(T // rows,),
        in_specs=[pl.BlockSpec((rows, D), lambda i: (i, 0)),
                  pl.BlockSpec((1, D), lambda i: (0, 0))],
        out_specs=pl.BlockSpec((rows, D), lambda i: (i, 0)),
        out_shape=jax.ShapeDtypeStruct((T, D), BF16),
        compiler_params=_cparams("parallel"),
        name="prenorm",
    )(x2, gain.reshape(1, D))


def _postnorm_kernel(x_ref, z_ref, gpost_ref, gnext_ref, xo_ref, h_ref):
    xn = x_ref[...] + _rms(z_ref[...].astype(F32), gpost_ref[...])
    xo_ref[...] = xn
    h_ref[...] = _rms(xn, gnext_ref[...]).astype(h_ref.dtype)


def _postnorm_last_kernel(x_ref, z_ref, gpost_ref, xo_ref):
    xo_ref[...] = x_ref[...] + _rms(z_ref[...].astype(F32), gpost_ref[...])


def postnorm(x2, z, gpost, gnext, *, rows=256):
    T, D = x2.shape
    row_spec = pl.BlockSpec((rows, D), lambda i: (i, 0))
    vec_spec = pl.BlockSpec((1, D), lambda i: (0, 0))
    if gnext is None:
        return pl.pallas_call(
            _postnorm_last_kernel,
            grid=(T // rows,),
            in_specs=[row_spec, row_spec, vec_spec],
            out_specs=row_spec,
            out_shape=jax.ShapeDtypeStruct((T, D), F32),
            compiler_params=_cparams("parallel"),
            name="postnorm_last",
        )(x2, z, gpost.reshape(1, D)), None
    return pl.pallas_call(
        _postnorm_kernel,
        grid=(T // rows,),
        in_specs=[row_spec, row_spec, vec_spec, vec_spec],
        out_specs=[row_spec, row_spec],
        out_shape=[jax.ShapeDtypeStruct((T, D), F32), jax.ShapeDtypeStruct((T, D), BF16)],
        compiler_params=_cparams("parallel"),
        name="postnorm",
    )(x2, z, gpost.reshape(1, D), gnext.reshape(1, D))


def _stream_weight_tiles(w_hbm, w16_ref, stage_ref, sem, *, layer, n_col_tiles, col0=0):
    j, i = pl.program_id(0), pl.program_id(1)
    _, k_total, bn = w16_ref.shape
    kc = stage_ref.shape[1]
    n_chunks = k_total // kc
    g = j * n_chunks + i
    n_steps = n_col_tiles * n_chunks

    def copy(tile, chunk, slot):
        return pltpu.make_async_copy(w_hbm.at[layer, pl.ds(chunk * kc, kc), pl.ds(col0 + tile * bn, bn)],
                                     stage_ref.at[slot], sem.at[slot])

    def stream_tile(step):
        return (1 + step // n_chunks) % n_col_tiles

    @pl.when(g == 0)
    def _first_tile():
        copy(0, 0, 0).start()
        for c in range(n_chunks):
            if c + 1 < n_chunks:
                copy(0, c + 1, (c + 1) % 2).start()
            copy(0, c, c % 2).wait()
            w16_ref[0, c * kc:(c + 1) * kc, :] = stage_ref[c % 2].astype(BF16)
        copy(stream_tile(0), 0, 0).start()

    @pl.when(g + 1 < n_steps)
    def _():
        copy(stream_tile(g + 1), (g + 1) % n_chunks, (g + 1) % 2).start()

    copy(stream_tile(g), i, g % 2).wait()
    w16_ref[(j + 1) % 2, pl.ds(pl.multiple_of(i * kc, kc), kc), :] = stage_ref[g % 2].astype(BF16)


def _mm_ws_kernel(a_ref, w_hbm, o_ref, w16_ref, stage_ref, sem, *, layer, n_col_tiles, col0, sigmoid_out):
    _stream_weight_tiles(w_hbm, w16_ref, stage_ref, sem, layer=layer, n_col_tiles=n_col_tiles, col0=col0)
    w16 = w16_ref[pl.program_id(0) % 2]
    acc = jnp.dot(a_ref[...], w16, preferred_element_type=F32)
    if sigmoid_out:
        acc = _sigmoid(acc)
    o_ref[...] = acc.astype(o_ref.dtype)


def _weight_stream_scratch(k_total, bn, n_row_tiles):
    assert k_total % n_row_tiles == 0 and n_row_tiles % 2 == 0
    return [pltpu.VMEM((2, k_total, bn), BF16), pltpu.VMEM((2, k_total // n_row_tiles, bn), F32),
            pltpu.SemaphoreType.DMA((2,))]


def matmul_ws(a, w_all, layer, *, bm, bn, out_dtype, name, col0=0, n_cols=None, sigmoid_out=False):
    M, K = a.shape
    N = w_all.shape[2] - col0 if n_cols is None else n_cols
    assert N % bn == 0 and M % bm == 0
    return pl.pallas_call(
        functools.partial(_mm_ws_kernel, layer=layer, n_col_tiles=N // bn, col0=col0, sigmoid_out=sigmoid_out),
        grid=(N // bn, M // bm),
        in_specs=[pl.BlockSpec((bm, K), lambda j, i: (i, 0)),
                  pl.BlockSpec(memory_space=pl.ANY)],
        out_specs=pl.BlockSpec((bm, bn), lambda j, i: (i, j)),
        out_shape=jax.ShapeDtypeStruct((M, N), out_dtype),
        scratch_shapes=_weight_stream_scratch(K, bn, M // bm),
        compiler_params=_cparams("arbitrary", "arbitrary"),
        name=name,
    )(a, w_all)


def _mm_kernel(a_ref, w_ref, o_ref):
    o_ref[...] = jnp.dot(a_ref[...], w_ref[...], preferred_element_type=F32).astype(o_ref.dtype)


def matmul_bf16(a, w, *, bm, bn, out_dtype, name):
    M, K = a.shape
    N = w.shape[1]
    return pl.pallas_call(
        _mm_kernel,
        grid=(M // bm, N // bn),
        in_specs=[pl.BlockSpec((bm, K), lambda i, j: (i, 0)),
                  pl.BlockSpec((K, bn), lambda i, j: (0, j))],
        out_specs=pl.BlockSpec((bm, bn), lambda i, j: (i, j)),
        out_shape=jax.ShapeDtypeStruct((M, N), out_dtype),
        compiler_params=_cparams("parallel", "arbitrary"),
        name=name,
    )(a, w)


def _cast_rows_stream(src_hbm, out_ref, stage_ref, sem, *, layer):
    j, i = pl.program_id(0), pl.program_id(1)
    n_row_tiles = pl.num_programs(1)
    g = j * n_row_tiles + i
    n_steps = pl.num_programs(0) * n_row_tiles
    r = stage_ref.shape[1]

    def copy(step, slot):
        return pltpu.make_async_copy(src_hbm.at[layer, pl.ds(step * r, r), :], stage_ref.at[slot], sem.at[slot])

    @pl.when(g == 0)
    def _():
        copy(0, 0).start()

    @pl.when(g + 1 < n_steps)
    def _():
        copy(g + 1, (g + 1) % 2).start()

    copy(g, g % 2).wait()
    out_ref[pl.ds(pl.multiple_of(i * r, r), r), :] = stage_ref[g % 2].astype(BF16)


def _swiglu_kernel(h_ref, wg_hbm, wu_hbm, wd_hbm, o_ref, wd16_ref,
                   wg16_ref, wg_stage, wg_sem, wu16_ref, wu_stage, wu_sem, wd_stage, wd_sem, *,
                   layer, n_col_tiles):
    _stream_weight_tiles(wg_hbm, wg16_ref, wg_stage, wg_sem, layer=layer, n_col_tiles=n_col_tiles)
    _stream_weight_tiles(wu_hbm, wu16_ref, wu_stage, wu_sem, layer=layer, n_col_tiles=n_col_tiles)
    _cast_rows_stream(wd_hbm, wd16_ref, wd_stage, wd_sem, layer=layer)
    slot = pl.program_id(0) % 2
    h = h_ref[...]
    a = jnp.dot(h, wg16_ref[slot], preferred_element_type=F32)
    b = jnp.dot(h, wu16_ref[slot], preferred_element_type=F32)
    o_ref[...] = (a * _sigmoid(a) * b).astype(o_ref.dtype)


def swiglu_up(h, wg_all, wu_all, wd_all, layer, *, bm, bn):
    M, K = h.shape
    F = wg_all.shape[2]
    D = wd_all.shape[2]
    n_row_tiles = M // bm
    assert F % bn == 0 and bn % n_row_tiles == 0
    any_spec = pl.BlockSpec(memory_space=pl.ANY)
    return pl.pallas_call(
        functools.partial(_swiglu_kernel, layer=layer, n_col_tiles=F // bn),
        grid=(F // bn, n_row_tiles),
        in_specs=[pl.BlockSpec((bm, K), lambda j, i: (i, 0)), any_spec, any_spec, any_spec],
        out_specs=[pl.BlockSpec((bm, bn), lambda j, i: (i, j)),
                   pl.BlockSpec((bn, D), lambda j, i: (j, 0))],
        out_shape=[jax.ShapeDtypeStruct((M, F), BF16), jax.ShapeDtypeStruct((F, D), BF16)],
        scratch_shapes=(_weight_stream_scratch(K, bn, n_row_tiles) + _weight_stream_scratch(K, bn, n_row_tiles)
                        + [pltpu.VMEM((2, bn // n_row_tiles, D), F32), pltpu.SemaphoreType.DMA((2,))]),
        compiler_params=_cparams("arbitrary", "arbitrary"),
        name="swiglu_up",
    )(h, wg_all, wu_all, wd_all)


def _merge_kernel(ya_ref, yb_ref, yc_ref, wa_hbm, wb_hbm, wc_hbm, ga_ref, gb_ref, gc_ref, o_ref,
                  wa16_ref, wa_stage, wa_sem, wb16_ref, wb_stage, wb_sem, wc16_ref, wc_stage, wc_sem, *,
                  layer, n_col_tiles):
    _stream_weight_tiles(wa_hbm, wa16_ref, wa_stage, wa_sem, layer=layer, n_col_tiles=n_col_tiles)
    _stream_weight_tiles(wb_hbm, wb16_ref, wb_stage, wb_sem, layer=layer, n_col_tiles=n_col_tiles)
    _stream_weight_tiles(wc_hbm, wc16_ref, wc_stage, wc_sem, layer=layer, n_col_tiles=n_col_tiles)
    slot = pl.program_id(0) % 2
    acc = ga_ref[...].astype(F32) * jnp.dot(ya_ref[...], wa16_ref[slot], preferred_element_type=F32)
    acc += gb_ref[...].astype(F32) * jnp.dot(yb_ref[...], wb16_ref[slot], preferred_element_type=F32)
    acc += gc_ref[...].astype(F32) * jnp.dot(yc_ref[...], wc16_ref[slot], preferred_element_type=F32)
    o_ref[...] = acc.astype(o_ref.dtype)


def gated_merge(ya, yb, yc, wa_all, wb_all, wc_all, layer, gates, d_model, *, bm, bn):
    M, Kb = ya.shape
    per_branch = d_model // bn
    n_row_tiles = M // bm
    y_spec = pl.BlockSpec((bm, Kb), lambda j, i: (i, 0))
    any_spec = pl.BlockSpec(memory_space=pl.ANY)

    def g_spec(k):
        return pl.BlockSpec((bm, bn), lambda j, i: (i, k * per_branch + j))

    return pl.pallas_call(
        functools.partial(_merge_kernel, layer=layer, n_col_tiles=per_branch),
        grid=(per_branch, n_row_tiles),
        in_specs=[y_spec, y_spec, y_spec, any_spec, any_spec, any_spec, g_spec(0), g_spec(1), g_spec(2)],
        out_specs=pl.BlockSpec((bm, bn), lambda j, i: (i, j)),
        out_shape=jax.ShapeDtypeStruct((M, d_model), BF16),
        scratch_shapes=_weight_stream_scratch(Kb, bn, n_row_tiles) * 3,
        compiler_params=_cparams("arbitrary", "arbitrary"),
        name="gated_merge",
    )(ya, yb, yc, wa_all, wb_all, wc_all, gates, gates, gates)


def _hgrn_tables(chunk):
    levels = int(math.log2(chunk))
    assert 1 << levels == chunk and levels >= 3
    t = np.arange(chunk)[:, None]
    u = np.arange(chunk)[None, :]
    masks = [(t == u)]
    for l in range(levels):
        m = 1 << l
        lower_t = (t % (2 * m)) >= m
        upper_u = (u % (2 * m)) < m
        masks.append(lower_t & upper_u & (t // (2 * m) == u // (2 * m)))
    return (u <= t).astype(np.float32), np.stack([m.astype(np.float32) for m in masks], axis=0)


def _split3_f32(x):
    p1 = x.astype(BF16).astype(F32)
    r1 = x - p1
    p2 = r1.astype(BF16).astype(F32)
    return p1, p2, r1 - p2


def _dot_nt(a, b):
    return lax.dot_general(a, b, (((1,), (1,)), ((), ())), preferred_element_type=F32)


def _hgrn_level_decays(f, b, chunk):
    dk = f.shape[1]
    row = lax.broadcasted_iota(jnp.int32, f.shape, 0)
    f_prev = pltpu.roll(f, 1, axis=0)
    f_next = pltpu.roll(f, chunk - 1, axis=0)
    r4 = row % 4
    decays = [jnp.where(row % 2 == 1, f, 1.0),
              jnp.where(r4 == 0, f_next, jnp.where(r4 == 1, 1.0, jnp.where(r4 == 2, f, f * f_prev)))]
    m = 4
    while m < chunk:
        ref = jnp.concatenate(
            [jnp.broadcast_to(b[p * 2 * m + m - 1:p * 2 * m + m, :], (2 * m, dk)) for p in range(chunk // (2 * m))],
            axis=0)
        lower = (row % (2 * m)) >= m
        decays.append(jnp.exp(jnp.where(lower, b - ref, ref - b)))
        m *= 2
    return decays


def _hgrn_kernel(q_ref, f_ref, v_ref, g_ref, lb_ref, gain_ref, tri_ref, masks_ref, o_ref, *,
                 chunk, n_chunks, unroll):
    dk = q_ref.shape[1]
    lb = lb_ref[...]
    gain = gain_ref[...]
    tri = tri_ref[...].astype(BF16)

    def body(c, st):
        rows = pl.ds(pl.multiple_of(c * chunk, chunk), chunk)
        q = q_ref[rows, :] * (HG_DK ** -0.5)
        f = lb + (1.0 - lb) * jax.nn.sigmoid(f_ref[rows, :])
        kk = 1.0 - f
        v16 = v_ref[rows, :].astype(BF16)
        parts = jnp.concatenate([p.astype(BF16) for p in _split3_f32(jnp.log(f))], axis=1)
        b3 = jnp.dot(tri, parts, preferred_element_type=F32)
        b = b3[:, :dk] + b3[:, dk:2 * dk] + b3[:, 2 * dk:]
        b_last = b[chunk - 1:chunk, :]
        q_inter = (q * jnp.exp(b)).astype(BF16)
        k_inter = (kk * jnp.exp(b_last - b)).astype(BF16)
        scores = masks_ref[0] * _dot_nt(q.astype(BF16), kk.astype(BF16)).astype(BF16)
        for l, xl in enumerate(_hgrn_level_decays(f, b, chunk)):
            scores += masks_ref[1 + l] * _dot_nt((q * xl).astype(BF16), (kk * xl).astype(BF16)).astype(BF16)
        o = jnp.dot(scores, v16, preferred_element_type=F32)
        o += _dot_nt(q_inter, st.astype(BF16))
        upd = lax.dot_general(v16, k_inter, (((0,), (0,)), ((), ())), preferred_element_type=F32)
        o = o * lax.rsqrt(jnp.mean(o * o, axis=-1, keepdims=True) + NORM_EPS) * gain
        gate = g_ref[rows, :]
        o_ref[rows, :] = (o * (gate * jax.nn.sigmoid(gate))).astype(o_ref.dtype)
        return st * jnp.exp(b_last) + upd

    lax.fori_loop(0, n_chunks, body, jnp.zeros((HG_DV, dk), F32), unroll=unroll)


def hgrn2_mixer(proj, lb, out_gain, batch, seq, *, chunk=256, unroll=8):
    T = proj.shape[0]
    tri, masks = _hgrn_tables(chunk)

    def col_spec(off):
        return pl.BlockSpec((seq, HG_DK), lambda b, h: (b, off // HG_DK + h))

    vec_spec = pl.BlockSpec((1, HG_DK), lambda b, h: (0, h))
    return pl.pallas_call(
        functools.partial(_hgrn_kernel, chunk=chunk, n_chunks=seq // chunk, unroll=unroll),
        grid=(batch, HG_HEADS),
        in_specs=[col_spec(OFF_HQ), col_spec(OFF_HF), col_spec(OFF_HV), col_spec(OFF_HG),
                  vec_spec, vec_spec,
                  pl.BlockSpec(tri.shape, lambda b, h: (0, 0)),
                  pl.BlockSpec(masks.shape, lambda b, h: (0, 0, 0))],
        out_specs=pl.BlockSpec((seq, HG_DV), lambda b, h: (b, h)),
        out_shape=jax.ShapeDtypeStruct((T, HG_WIDTH), BF16),
        compiler_params=_cparams("parallel", "parallel"),
        name="hgrn2",
    )(proj, proj, proj, proj, lb.reshape(1, -1), out_gain.reshape(1, -1),
      jnp.asarray(tri), jnp.asarray(masks, dtype=BF16))


def _pool_kernel(u_ref, w_ref, scale_ref, o_ref):
    seq = u_ref.shape[0]
    row = lax.broadcasted_iota(jnp.int32, (seq, POOL_GROUP_DIM), 0)
    for j, window in enumerate(POOL_WINDOWS):
        cols = slice(j * POOL_GROUP_DIM, (j + 1) * POOL_GROUP_DIM)
        u = u_ref[:, cols]
        acc = u
        span = 1
        while span < window:
            shifted = jnp.where(row >= span, pltpu.roll(acc, span, axis=0), 0.0)
            acc = acc + shifted
            span *= 2
        count = jnp.minimum(row + 1, window).astype(F32)
        pooled = acc / count - u
        mixed = jnp.dot(pooled.astype(BF16), w_ref[j].astype(BF16), preferred_element_type=F32)
        o_ref[:, cols] = (mixed * scale_ref[:, cols]).astype(o_ref.dtype)


def pool_mixer(proj, w_groups, scale, batch, seq):
    T = proj.shape[0]
    assert all(w & (w - 1) == 0 for w in POOL_WINDOWS)
    return pl.pallas_call(
        _pool_kernel,
        grid=(batch,),
        in_specs=[pl.BlockSpec((seq, POOL_WIDTH), lambda b: (b, OFF_PU // POOL_WIDTH)),
                  pl.BlockSpec(w_groups.shape, lambda b: (0, 0, 0)),
                  pl.BlockSpec((1, POOL_WIDTH), lambda b: (0, 0))],
        out_specs=pl.BlockSpec((seq, POOL_WIDTH), lambda b: (b, 0)),
        out_shape=jax.ShapeDtypeStruct((T, POOL_WIDTH), BF16),
        compiler_params=_cparams("parallel"),
        name="pool",
    )(proj, w_groups, scale.reshape(1, -1))


DA_AUG = 2 * DA_VDIM


def _diffattn_kernel(slopes_ref, q_ref, k_ref, v_ref, lam_ref, subln_ref, o_ref, ka_ref, va_ref, s_ref, *,
                     seq, blk, lambda_init):
    slope = slopes_ref[pl.program_id(1)]
    lp = lam_ref[...]
    lam = (jnp.exp(jnp.sum(lp[0:1] * lp[1:2], axis=-1, keepdims=True))
           - jnp.exp(jnp.sum(lp[2:3] * lp[3:4], axis=-1, keepdims=True)) + lambda_init)

    qlane = lax.broadcasted_iota(jnp.int32, (blk, DA_VDIM), 1)
    block_row = lax.broadcasted_iota(jnp.int32, (blk, DA_VDIM), 0)
    ones3 = jnp.where(qlane < 3, 1.0, 0.0).astype(BF16)
    ones_col = jnp.where(qlane == 0, 1.0, 0.0).astype(BF16)
    key_gt_query = (lax.broadcasted_iota(jnp.int32, (2 * blk, blk), 1)
                    > lax.broadcasted_iota(jnp.int32, (2 * blk, blk), 0) % blk)

    for i in range(seq // blk):
        rows = pl.ds(i * blk, blk)
        p1, p2, p3 = _split3_f32((block_row + i * blk).astype(F32) * (slope * LOG2E))
        ka_ref[rows, 0:DA_VDIM] = k_ref[rows, :].astype(BF16)
        ka_ref[rows, DA_VDIM:DA_AUG] = jnp.where(
            qlane == 0, p1, jnp.where(qlane == 1, p2, jnp.where(qlane == 2, p3, 0.0))).astype(BF16)
        va_ref[rows, 0:DA_VDIM] = v_ref[rows, :].astype(BF16)
        va_ref[rows, DA_VDIM:DA_AUG] = ones_col
        q = q_ref[rows, :] * (DA_HEAD_DIM ** -0.5 * LOG2E)
        q2 = jnp.concatenate(
            [jnp.concatenate([jnp.where(qlane < DA_HEAD_DIM, q, 0.0).astype(BF16), ones3], axis=1),
             jnp.concatenate([jnp.where(qlane >= DA_HEAD_DIM, q, 0.0).astype(BF16), ones3], axis=1)],
            axis=0)
        m_tile = jnp.full((2 * blk, DA_VDIM), -jnp.inf, F32)
        for j in range(i + 1):
            keys = pl.ds(j * blk, blk)
            s = _dot_nt(q2, ka_ref[keys, :])
            if j == i:
                s = jnp.where(key_gt_query, -jnp.inf, s)
            s_ref[:, keys] = s
            for c in range(blk // DA_VDIM):
                m_tile = jnp.maximum(m_tile, s[:, c * DA_VDIM:(c + 1) * DA_VDIM])
        m_row = jnp.max(m_tile, axis=-1, keepdims=True)
        acc = jnp.zeros((2 * blk, DA_AUG), F32)
        for j in range(i + 1):
            keys = pl.ds(j * blk, blk)
            p = jnp.exp2(s_ref[:, keys] - m_row).astype(BF16)
            acc += jnp.dot(p, va_ref[keys, :], preferred_element_type=F32)
        out0 = acc[0:blk, 0:DA_VDIM] * (1.0 / acc[0:blk, DA_VDIM:DA_VDIM + 1])
        out1 = acc[blk:2 * blk, 0:DA_VDIM] * (1.0 / acc[blk:2 * blk, DA_VDIM:DA_VDIM + 1])
        o = out0 - lam * out1
        o = o * lax.rsqrt(jnp.mean(o * o, axis=-1, keepdims=True) + NORM_EPS) * subln_ref[...]
        o_ref[rows, :] = (o * (1.0 - lambda_init)).astype(o_ref.dtype)


def diff_attention(proj, lam_params, subln, lambda_init, batch, seq, *, blk=256):
    T = proj.shape[0]
    slopes = (2.0 ** (-8.0 * jnp.arange(1, DA_HEADS + 1, dtype=F32) / DA_HEADS)).astype(F32)

    def col_spec(off):
        return pl.BlockSpec((seq, DA_VDIM), lambda b, h, s: (b, off // DA_VDIM + h))

    grid_spec = pltpu.PrefetchScalarGridSpec(
        num_scalar_prefetch=1,
        grid=(batch, DA_HEADS),
        in_specs=[col_spec(OFF_DQ), col_spec(OFF_DK), col_spec(OFF_DV),
                  pl.BlockSpec(lam_params.shape, lambda b, h, s: (0, 0)),
                  pl.BlockSpec((1, DA_VDIM), lambda b, h, s: (0, 0))],
        out_specs=pl.BlockSpec((seq, DA_VDIM), lambda b, h, s: (b, h)),
        scratch_shapes=[pltpu.VMEM((seq, DA_AUG), BF16), pltpu.VMEM((seq, DA_AUG), BF16),
                        pltpu.VMEM((2 * blk, seq), F32)],
    )
    return pl.pallas_call(
        functools.partial(_diffattn_kernel, seq=seq, blk=blk, lambda_init=lambda_init),
        grid_spec=grid_spec,
        out_shape=jax.ShapeDtypeStruct((T, DA_WIDTH), BF16),
        compiler_params=_cparams("parallel", "parallel"),
        name="diffattn",
    )(slopes, proj, proj, proj, lam_params, subln.reshape(1, -1))


def kernel(x, norm_mix_pre, norm_mix_post, norm_ffn_pre, norm_ffn_post, w_in, hgrn_lb_logits, hgrn_out_norm,
           pool_w, pool_scale, diff_lambda, diff_subln, w_up_a, w_up_b, w_up_c, w_out, w_ffn_gate, w_ffn_up,
           w_ffn_down):
    B, S, D = x.shape
    depth = w_in.shape[0]
    T = B * S
    lb_all = jnp.cumsum(jax.nn.softmax(hgrn_lb_logits.astype(F32), axis=0), axis=0)
    lb_all = lb_all - lb_all[0:1]

    x2 = x.reshape(T, D)
    h = prenorm(x2, norm_mix_pre[0])
    for l in range(depth):
        lambda_init = 0.8 - 0.6 * math.exp(-0.3 * l)
        proj = matmul_ws(h, w_in, l, bm=1024, bn=1024, out_dtype=F32, name="in_proj_mix", n_cols=OFF_GATE)
        gates = matmul_ws(h, w_in, l, bm=1024, bn=1024, out_dtype=BF16, name="in_proj_gate", col0=OFF_GATE,
                          sigmoid_out=True)
        y_a = hgrn2_mixer(proj, lb_all[l], hgrn_out_norm[l], B, S)
        y_b = pool_mixer(proj, pool_w[l], pool_scale[l], B, S)
        y_c = diff_attention(proj, diff_lambda[l], diff_subln[l], lambda_init, B, S)
        merged = gated_merge(y_a, y_b, y_c, w_up_a, w_up_b, w_up_c, l, gates, D, bm=1024, bn=1024)
        z = matmul_ws(merged, w_out, l, bm=1024, bn=1024, out_dtype=BF16, name="out_proj")
        x2, h = postnorm(x2, z, norm_mix_post[l], norm_ffn_pre[l])
        u, wd16 = swiglu_up(h, w_ffn_gate, w_ffn_up, w_ffn_down, l, bm=2048, bn=256)
        ff = matmul_bf16(u, wd16, bm=512, bn=512, out_dtype=BF16, name="ffn_down")
        gnext = norm_mix_pre[l + 1] if l + 1 < depth else None
        x2, h = postnorm(x2, ff, norm_ffn_post[l], gnext)
    return x2.reshape(B, S, D)
```
